```python
import math
import numpy as np
import jax
import jax.numpy as jnp
from jax import lax

D_MODEL = 1024
BATCH = 16
SEQ = 2048
DEPTH = 2

S5_GROUP = 16
S5_CHANNELS = D_MODEL // 4
S5_GROUPS = S5_CHANNELS // S5_GROUP
S5_STATE = 64
SB_HEAD_DIM = 64
SB_HEADS = (D_MODEL - S5_CHANNELS) // SB_HEAD_DIM
SB_WIDTH = SB_HEADS * SB_HEAD_DIM
EVEN_IN = S5_CHANNELS + 3 * SB_WIDTH
EVEN_MIX = S5_CHANNELS + SB_WIDTH
Q_BLOCK = 128
NSA_HEADS = 16
NSA_KV_HEADS = 4
NSA_GROUP = NSA_HEADS // NSA_KV_HEADS
NSA_HEAD_DIM = 64
NSA_Q_WIDTH = NSA_HEADS * NSA_HEAD_DIM
NSA_KV_WIDTH = NSA_KV_HEADS * NSA_HEAD_DIM
NSA_BRANCHES = 3
ODD_IN = NSA_Q_WIDTH + 2 * NSA_BRANCHES * NSA_KV_WIDTH + NSA_BRANCHES * NSA_HEADS
CMP_BLOCK = 32
CMP_STRIDE = 16
CMP_HIDDEN = 128
SEL_BLOCK = 64
SEL_TOPK = 16
SEL_Q_CHUNK = 16
WINDOW = 512
FORCE_BONUS = 1000.0
PEER_HEADS = 8
PEER_KEYS = 128
PEER_EXPERTS = PEER_KEYS * PEER_KEYS
PEER_KEY_DIM = 256
PEER_HALF = PEER_KEY_DIM // 2
PEER_TOPK = 16
PEER_CHUNK = 128
DN_ALPHA = (2 * DEPTH) ** 0.25
DN_BETA = (8 * DEPTH) ** -0.25
LN_EPS = 1e-5
N_EVEN = (DEPTH + 1) // 2
N_ODD = DEPTH // 2

kernel_name = 'hybrid_s5_stickbreak_nsa_peer_deepnorm'


def layer_norm(x, g, b):
    xf = x.astype(jnp.float32)
    mu = jnp.mean(xf, axis=-1, keepdims=True)
    var = jnp.mean(jnp.square(xf - mu), axis=-1, keepdims=True)
    return ((xf - mu) * lax.rsqrt(var + LN_EPS) * g + b).astype(x.dtype)


def masked_softmax(s, mask):
    s = jnp.where(mask, s, -jnp.inf)
    m = jnp.max(s, axis=-1, keepdims=True)
    m = jnp.where(jnp.isfinite(m), m, 0.0)
    e = jnp.where(mask, jnp.exp(s - m), 0.0)
    return e / jnp.maximum(jnp.sum(e, axis=-1, keepdims=True), 1e-30)


def alibi_slopes(n):
    return jnp.asarray(2.0 ** (-8.0 * np.arange(1, n + 1) / n), jnp.float32)


def cmp_to_sel_matrix(n_cmp, n_sel):
    cs = np.arange(n_cmp)[:, None] * CMP_STRIDE
    ss = np.arange(n_sel)[None, :] * SEL_BLOCK
    ov = np.clip(np.minimum(cs + CMP_BLOCK, ss + SEL_BLOCK) - np.maximum(cs, ss), 0, None) / CMP_BLOCK
    return jnp.asarray(ov, jnp.float32)


def s5_mixer(u, lam_re, lam_im, log_dt, b_re, b_im, c_re, c_im, d_skip, w_glu, b_glu):
    f32 = jnp.float32
    bsz, L, _ = u.shape
    uf = u.astype(f32)
    ug = uf.reshape(bsz, L, S5_GROUPS, S5_GROUP)
    lr = lam_re.astype(f32)
    li = lam_im.astype(f32)
    dt = jnp.exp(log_dt.astype(f32))[:, None]
    mag = jnp.exp(lr * dt)
    ab_re = mag * jnp.cos(li * dt)
    ab_im = mag * jnp.sin(li * dt)
    nr = ab_re - 1.0
    den = lr * lr + li * li
    f_re = (nr * lr + ab_im * li) / den
    f_im = (ab_im * lr - nr * li) / den
    bb_re = f_re[..., None] * b_re.astype(f32) - f_im[..., None] * b_im.astype(f32)
    bb_im = f_re[..., None] * b_im.astype(f32) + f_im[..., None] * b_re.astype(f32)
    bu_re = jnp.einsum('blgi,gpi->blgp', ug, bb_re)
    bu_im = jnp.einsum('blgi,gpi->blgp', ug, bb_im)
    a_re = jnp.broadcast_to(ab_re, bu_re.shape)
    a_im = jnp.broadcast_to(ab_im, bu_im.shape)

    def combine(e1, e2):
        a1r, a1i, b1r, b1i = e1
        a2r, a2i, b2r, b2i = e2
        return (a2r * a1r - a2i * a1i,
                a2r * a1i + a2i * a1r,
                a2r * b1r - a2i * b1i + b2r,
                a2r * b1i + a2i * b1r + b2i)

    _, _, x_re, x_im = lax.associative_scan(combine, (a_re, a_im, bu_re, bu_im), axis=1)
    y = (jnp.einsum('blgp,gip->blgi', x_re, c_re.astype(f32))
         - jnp.einsum('blgp,gip->blgi', x_im, c_im.astype(f32)))
    y = y.reshape(bsz, L, S5_CHANNELS) + d_skip.astype(f32) * uf
    y = jax.nn.gelu(y)
    y = y * jax.nn.sigmoid(y @ w_glu.astype(f32) + b_glu.astype(f32))
    return y.astype(u.dtype)


def stick_breaking_attention(q, k, v):
    L = q.shape[1]
    scale = SB_HEAD_DIM ** -0.5
    outs = []
    for blk in range(L // Q_BLOCK):
        t0 = blk * Q_BLOCK
        t1 = t0 + Q_BLOCK
        z = jnp.einsum('bqhd,bkhd->bhqk', q[:, t0:t1], k[:, :t1]).astype(jnp.float32) * scale
        tq = t0 + jnp.arange(Q_BLOCK)[:, None]
        sk = jnp.arange(t1)[None, :]
        mask = sk < tq
        log1m = jnp.where(mask, jax.nn.log_sigmoid(-z), 0.0)
        tail = lax.cumsum(log1m, axis=3, reverse=True) - log1m
        w = jnp.where(mask, jnp.exp(jax.nn.log_sigmoid(z) + tail), 0.0)
        outs.append(jnp.einsum('bhqk,bkhd->bqhd', w.astype(v.dtype), v[:, :t1]))
    return jnp.concatenate(outs, axis=1)


def even_mixer(x, w_in, lam_re, lam_im, log_dt, b_re, b_im, c_re, c_im, d_skip, w_glu, b_glu, w_out):
    bsz, L, _ = x.shape
    proj = x @ w_in
    u = proj[..., :S5_CHANNELS]
    o = S5_CHANNELS
    q = proj[..., o:o + SB_WIDTH].reshape(bsz, L, SB_HEADS, SB_HEAD_DIM)
    k = proj[..., o + SB_WIDTH:o + 2 * SB_WIDTH].reshape(bsz, L, SB_HEADS, SB_HEAD_DIM)
    v = proj[..., o + 2 * SB_WIDTH:].reshape(bsz, L, SB_HEADS, SB_HEAD_DIM)
    y_a = s5_mixer(u, lam_re, lam_im, log_dt, b_re, b_im, c_re, c_im, d_skip, w_glu, b_glu)
    y_b = stick_breaking_attention(q, k, v).reshape(bsz, L, SB_WIDTH).astype(x.dtype)
    return jnp.concatenate([y_a, y_b], axis=-1) @ w_out


def nsa_mixer(x, w_in, pe_k, w1_k, w2_k, pe_v, w1_v, w2_v, w_out):
    f32 = jnp.float32
    bsz, L, _ = x.shape
    G, R, hd = NSA_KV_HEADS, NSA_GROUP, NSA_HEAD_DIM
    scale = hd ** -0.5
    proj = x @ w_in
    q = proj[..., :NSA_Q_WIDTH].reshape(bsz, L, G, R, hd)
    kv = proj[..., NSA_Q_WIDTH:NSA_Q_WIDTH + 2 * NSA_BRANCHES * NSA_KV_WIDTH].reshape(bsz, L, 2 * NSA_BRANCHES, G, hd)
    kc, vc, ks, vs, kw, vw = [kv[:, :, j] for j in range(2 * NSA_BRANCHES)]
    gates = jax.nn.sigmoid(proj[..., NSA_Q_WIDTH + 2 * NSA_BRANCHES * NSA_KV_WIDTH:].astype(f32))
    gates = gates.reshape(bsz, L, NSA_BRANCHES, G, R)
    slopes = alibi_slopes(NSA_HEADS).reshape(G, R)
    pos = jnp.arange(L)

    n_cmp = (L - CMP_BLOCK) // CMP_STRIDE + 1
    starts = jnp.arange(n_cmp) * CMP_STRIDE
    idx = starts[:, None] + jnp.arange(CMP_BLOCK)[None, :]

    def compress(t, pe, w1, w2):
        blocks = t[:, idx] + pe[None, None, :, None, :]
        blocks = jnp.moveaxis(blocks, 3, 2).reshape(bsz, n_cmp, G, CMP_BLOCK * hd)
        return jax.nn.gelu(blocks @ w1) @ w2

    k_cmp = compress(kc, pe_k, w1_k, w2_k)
    v_cmp = compress(vc, pe_v, w1_v, w2_v)
    ends = starts + CMP_BLOCK - 1
    dist_c = (pos[:, None] - ends[None, :]).astype(f32)
    s_c = jnp.einsum('btgrd,bngd->bgrtn', q, k_cmp).astype(f32) * scale
    s_c = s_c - slopes[:, :, None, None] * dist_c
    p_c = masked_softmax(s_c, dist_c >= 0)
    o_cmp = jnp.einsum('bgrtn,bngd->btgrd', p_c.astype(v_cmp.dtype), v_cmp)

    n_sel = L // SEL_BLOCK
    k_sel = min(SEL_TOPK, n_sel)
    imp = jnp.einsum('bgrtn,nj->bgtj', p_c, cmp_to_sel_matrix(n_cmp, n_sel))
    jb = jnp.arange(n_sel)[None, :]
    cur = (pos // SEL_BLOCK)[:, None]
    causal = jb * SEL_BLOCK <= pos[:, None]
    forced = (jb == 0) | (jb == cur) | (jb == cur - 1)
    imp = jnp.where(causal, imp + jnp.where(forced, FORCE_BONUS, 0.0), -jnp.inf)
    _, sel_idx = lax.top_k(imp, k_sel)

    ks_blk = jnp.moveaxis(ks.reshape(bsz, n_sel, SEL_BLOCK, G, hd), 3, 1)
    vs_blk = jnp.moveaxis(vs.reshape(bsz, n_sel, SEL_BLOCK, G, hd), 3, 1)
    gather_bg = jax.vmap(jax.vmap(lambda tbl, ix: tbl[ix]))
    n_chunk = L // SEL_Q_CHUNK
    q_ch = jnp.moveaxis(q.reshape(bsz, n_chunk, SEL_Q_CHUNK, G, R, hd), 1, 0)
    idx_ch = jnp.moveaxis(sel_idx.reshape(bsz, G, n_chunk, SEL_Q_CHUNK, k_sel), 2, 0)
    pos_ch = pos.reshape(n_chunk, SEL_Q_CHUNK)
    m_keys = k_sel * SEL_BLOCK

    def sel_chunk(args):
        qc, ic, tc = args
        kg = gather_bg(ks_blk, ic).reshape(bsz, G, SEL_Q_CHUNK, m_keys, hd)
        vg = gather_bg(vs_blk, ic).reshape(bsz, G, SEL_Q_CHUNK, m_keys, hd)
        kpos = (ic[..., None] * SEL_BLOCK + jnp.arange(SEL_BLOCK)).reshape(bsz, G, SEL_Q_CHUNK, m_keys)
        dist = (tc[None, None, :, None] - kpos).astype(f32)
        s = jnp.einsum('bcgrd,bgcmd->bgrcm', qc, kg).astype(f32) * scale
        s = s - slopes[None, :, :, None, None] * dist[:, :, None]
        p = masked_softmax(s, dist[:, :, None] >= 0)
        return jnp.einsum('bgrcm,bgcmd->bcgrd', p.astype(vg.dtype), vg)

    o_slc = lax.map(sel_chunk, (q_ch, idx_ch, pos_ch))
    o_slc = jnp.moveaxis(o_slc, 0, 1).reshape(bsz, L, G, R, hd)

    kw_pad = jnp.pad(kw, ((0, 0), (WINDOW, 0), (0, 0), (0, 0)))
    vw_pad = jnp.pad(vw, ((0, 0), (WINDOW, 0), (0, 0), (0, 0)))
    n_qb = L // Q_BLOCK
    span = WINDOW + Q_BLOCK
    q_b = jnp.moveaxis(q.reshape(bsz, n_qb, Q_BLOCK, G, R, hd), 1, 0)

    def win_block(args):
        qb, i = args
        t0 = i * Q_BLOCK
        kb = lax.dynamic_slice_in_dim(kw_pad, t0, span, axis=1)
        vb = lax.dynamic_slice_in_dim(vw_pad, t0, span, axis=1)
        tq = t0 + jnp.arange(Q_BLOCK)
        sk = t0 - WINDOW + jnp.arange(span)
        dist = tq[:, None] - sk[None, :]
        mask = (dist >= 0) & (dist < WINDOW) & (sk[None, :] >= 0)
        s = jnp.einsum('bqgrd,bkgd->bgrqk', qb, kb).astype(f32) * scale
        s = s - slopes[:, :, None, None] * dist.astype(f32)
        p = masked_softmax(s, mask)
        return jnp.einsum('bgrqk,bkgd->bqgrd', p.astype(vb.dtype), vb)

    o_win = lax.map(win_block, (q_b, jnp.arange(n_qb)))
    o_win = jnp.moveaxis(o_win, 0, 1).reshape(bsz, L, G, R, hd)

    o = (gates[:, :, 0, :, :, None] * o_cmp + gates[:, :, 1, :, :, None] * o_slc
         + gates[:, :, 2, :, :, None] * o_win)
    return o.reshape(bsz, L, NSA_Q_WIDTH).astype(x.dtype) @ w_out


def peer_ffn(x, w_q, sub_keys, u_tab, v_tab):
    f32 = jnp.float32
    bsz, L, D = x.shape
    tok = x.reshape(-1, PEER_CHUNK, D)

    def chunk(xc):
        q = (xc @ w_q).reshape(PEER_CHUNK, PEER_HEADS, 2, PEER_HALF)
        s1 = jnp.einsum('chd,hkd->chk', q[:, :, 0], sub_keys[:, 0]).astype(f32)
        s2 = jnp.einsum('chd,hkd->chk', q[:, :, 1], sub_keys[:, 1]).astype(f32)
        v1, i1 = lax.top_k(s1, PEER_TOPK)
        v2, i2 = lax.top_k(s2, PEER_TOPK)
        cand = (v1[..., :, None] + v2[..., None, :]).reshape(PEER_CHUNK, PEER_HEADS, PEER_TOPK * PEER_TOPK)
        top, flat = lax.top_k(cand, PEER_TOPK)
        e1 = jnp.take_along_axis(i1, flat // PEER_TOPK, axis=-1)
        e2 = jnp.take_along_axis(i2, flat % PEER_TOPK, axis=-1)
        expert = e1 * PEER_KEYS + e2
        g = jax.nn.softmax(top, axis=-1)
        h = jax.nn.gelu(jnp.einsum('cd,chkd->chk', xc, u_tab[expert]).astype(f32))
        return jnp.einsum('chk,chkd->cd', (g * h).astype(v_tab.dtype), v_tab[expert])

    return lax.map(chunk, tok).reshape(bsz, L, D)


def setup_inputs(seed: int = 0) -> dict:
    key = jax.random.key(seed)
    ks = iter(jax.random.split(key, 48))
    f32 = jnp.float32

    def nrm(shape, std):
        return jax.random.normal(next(ks), shape, f32) * std

    D = D_MODEL
    x = nrm((BATCH, SEQ, D), 1.0)
    even_w_in = nrm((N_EVEN, D, EVEN_IN), D ** -0.5)
    even_lam_re = -0.5 + nrm((N_EVEN, S5_GROUPS, S5_STATE), 0.01)
    even_lam_im = math.pi * jnp.arange(S5_STATE, dtype=f32) + nrm((N_EVEN, S5_GROUPS, S5_STATE), 0.01)
    even_log_dt = jax.random.uniform(next(ks), (N_EVEN, S5_GROUPS), f32, math.log(1e-3), math.log(1e-1))
    b_std = (2.0 * S5_GROUP) ** -0.5
    even_b_re = nrm((N_EVEN, S5_GROUPS, S5_STATE, S5_GROUP), b_std)
    even_b_im = nrm((N_EVEN, S5_GROUPS, S5_STATE, S5_GROUP), b_std)
    c_std = (0.5 * S5_STATE) ** -0.5
    even_c_re = nrm((N_EVEN, S5_GROUPS, S5_GROUP, S5_STATE), c_std)
    even_c_im = nrm((N_EVEN, S5_GROUPS, S5_GROUP, S5_STATE), c_std)
    even_d = nrm((N_EVEN, S5_CHANNELS), 1.0)
    even_w_glu = nrm((N_EVEN, S5_CHANNELS, S5_CHANNELS), S5_CHANNELS ** -0.5)
    even_b_glu = nrm((N_EVEN, S5_CHANNELS), 0.02)
    even_w_out = nrm((N_EVEN, EVEN_MIX, D), EVEN_MIX ** -0.5 * DN_BETA)
    odd_w_in = nrm((N_ODD, D, ODD_IN), D ** -0.5)
    odd_pe_k = nrm((N_ODD, CMP_BLOCK, NSA_HEAD_DIM), 0.1)
    odd_w1_k = nrm((N_ODD, CMP_BLOCK * NSA_HEAD_DIM, CMP_HIDDEN), (CMP_BLOCK * NSA_HEAD_DIM) ** -0.5)
    odd_w2_k = nrm((N_ODD, CMP_HIDDEN, NSA_HEAD_DIM), CMP_HIDDEN ** -0.5)
    odd_pe_v = nrm((N_ODD, CMP_BLOCK, NSA_HEAD_DIM), 0.1)
    odd_w1_v = nrm((N_ODD, CMP_BLOCK * NSA_HEAD_DIM, CMP_HIDDEN), (CMP_BLOCK * NSA_HEAD_DIM) ** -0.5)
    odd_w2_v = nrm((N_ODD, CMP_HIDDEN, NSA_HEAD_DIM), CMP_HIDDEN ** -0.5)
    odd_w_out = nrm((N_ODD, NSA_Q_WIDTH, D), NSA_Q_WIDTH ** -0.5 * DN_BETA)
    peer_w_q = nrm((DEPTH, D, PEER_HEADS * PEER_KEY_DIM), D ** -0.5)
    peer_sub_keys = nrm((DEPTH, PEER_HEADS, 2, PEER_KEYS, PEER_HALF), PEER_HALF ** -0.5)
    peer_u = nrm((DEPTH, PEER_EXPERTS, D), D ** -0.5)
    peer_v = nrm((DEPTH, PEER_EXPERTS, D), DN_BETA)
    ln1_g = 1.0 + nrm((DEPTH, D), 0.02)
    ln1_b = nrm((DEPTH, D), 0.02)
    ln2_g = 1.0 + nrm((DEPTH, D), 0.02)
    ln2_b = nrm((DEPTH, D), 0.02)
    return {'x': x, 'even_w_in': even_w_in, 'even_lam_re': even_lam_re, 'even_lam_im': even_lam_im,
            'even_log_dt': even_log_dt, 'even_b_re': even_b_re, 'even_b_im': even_b_im,
            'even_c_re': even_c_re, 'even_c_im': even_c_im, 'even_d': even_d,
            'even_w_glu': even_w_glu, 'even_b_glu': even_b_glu, 'even_w_out': even_w_out,
            'odd_w_in': odd_w_in, 'odd_pe_k': odd_pe_k, 'odd_w1_k': odd_w1_k, 'odd_w2_k': odd_w2_k,
            'odd_pe_v': odd_pe_v, 'odd_w1_v': odd_w1_v, 'odd_w2_v': odd_w2_v, 'odd_w_out': odd_w_out,
            'peer_w_q': peer_w_q, 'peer_sub_keys': peer_sub_keys, 'peer_u': peer_u, 'peer_v': peer_v,
            'ln1_g': ln1_g, 'ln1_b': ln1_b, 'ln2_g': ln2_g, 'ln2_b': ln2_b}


def reference(x, even_w_in, even_lam_re, even_lam_im, even_log_dt, even_b_re, even_b_im,
              even_c_re, even_c_im, even_d, even_w_glu, even_b_glu, even_w_out,
              odd_w_in, odd_pe_k, odd_w1_k, odd_w2_k, odd_pe_v, odd_w1_v, odd_w2_v, odd_w_out,
              peer_w_q, peer_sub_keys, peer_u, peer_v, ln1_g, ln1_b, ln2_g, ln2_b):
    h = x
    for i in range(DEPTH):
        j = i // 2
        if i % 2 == 0:
            mix = even_mixer(h, even_w_in[j], even_lam_re[j], even_lam_im[j], even_log_dt[j],
                             even_b_re[j], even_b_im[j], even_c_re[j], even_c_im[j], even_d[j],
                             even_w_glu[j], even_b_glu[j], even_w_out[j])
        else:
            mix = nsa_mixer(h, odd_w_in[j], odd_pe_k[j], odd_w1_k[j], odd_w2_k[j],
                            odd_pe_v[j], odd_w1_v[j], odd_w2_v[j], odd_w_out[j])
        h = layer_norm(DN_ALPHA * h + mix, ln1_g[i], ln1_b[i])
        ffn = peer_ffn(h, peer_w_q[i], peer_sub_keys[i], peer_u[i], peer_v[i])
        h = layer_norm(DN_ALPHA * h + ffn, ln2_g[i], ln2_b[i])
    return h
```

```python
import functools
import math

import numpy as np
import jax
import jax.numpy as jnp
from jax import lax
from jax.experimental import pallas as pl
from jax.experimental.pallas import tpu as pltpu

D_MODEL = 1024
DEPTH = 2
S5_GROUP = 16
S5_CHANNELS = D_MODEL // 4
S5_GROUPS = S5_CHANNELS // S5_GROUP
S5_STATE = 64
SB_HEAD_DIM = 64
SB_HEADS = (D_MODEL - S5_CHANNELS) // SB_HEAD_DIM
SB_WIDTH = SB_HEADS * SB_HEAD_DIM
Q_BLOCK = 128
NSA_HEADS = 16
NSA_KV_HEADS = 4
NSA_GROUP = NSA_HEADS // NSA_KV_HEADS
NSA_HEAD_DIM = 64
NSA_Q_WIDTH = NSA_HEADS * NSA_HEAD_DIM
NSA_KV_WIDTH = NSA_KV_HEADS * NSA_HEAD_DIM
NSA_BRANCHES = 3
CMP_BLOCK = 32
CMP_STRIDE = 16
SEL_BLOCK = 64
SEL_TOPK = 16
SEL_Q_CHUNK = 16
WINDOW = 512
FORCE_BONUS = 1000.0
PEER_HEADS = 8
PEER_KEYS = 128
PEER_KEY_DIM = 256
PEER_HALF = PEER_KEY_DIM // 2
PEER_TOPK = 16
PEER_CHUNK = 128
DN_ALPHA = (2 * DEPTH) ** 0.25
LN_EPS = 1e-5

LANE = 128


def _mm_kernel(x_ref, w_ref, o_ref):
    o_ref[...] = jnp.dot(x_ref[...].astype(jnp.bfloat16), w_ref[...].astype(jnp.bfloat16),
                         preferred_element_type=jnp.float32)


def _matmul(x, w, tm=512, tn=512):
    m, k = x.shape
    n = w.shape[1]
    n_pad = -(-n // LANE) * LANE
    if n_pad != n:
        w = jnp.pad(w, ((0, 0), (0, n_pad - n)))
    while n_pad % tn:
        tn //= 2
    assert m % tm == 0 and tn >= LANE
    out = pl.pallas_call(
        _mm_kernel,
        grid=(m // tm, n_pad // tn),
        in_specs=[pl.BlockSpec((tm, k), lambda i, j: (i, 0)),
                  pl.BlockSpec((k, tn), lambda i, j: (0, j))],
        out_specs=pl.BlockSpec((tm, tn), lambda i, j: (i, j)),
        out_shape=jax.ShapeDtypeStruct((m, n_pad), jnp.float32),
        name="matmul",
    )(x, w)
    return out[:, :n] if n_pad != n else out


def _proj(x, w):
    b, l, d = x.shape
    return _matmul(x.reshape(b * l, d), w).reshape(b, l, w.shape[1])


def layer_norm(x, g, b):
    xf = x.astype(jnp.float32)
    mu = jnp.mean(xf, axis=-1, keepdims=True)
    var = jnp.mean(jnp.square(xf - mu), axis=-1, keepdims=True)
    return ((xf - mu) * lax.rsqrt(var + LN_EPS) * g + b).astype(x.dtype)


def masked_softmax(s, mask):
    s = jnp.where(mask, s, -jnp.inf)
    m = jnp.max(s, axis=-1, keepdims=True)
    m = jnp.where(jnp.isfinite(m), m, 0.0)
    e = jnp.where(mask, jnp.exp(s - m), 0.0)
    return e / jnp.maximum(jnp.sum(e, axis=-1, keepdims=True), 1e-30)


def alibi_slopes(n):
    return jnp.asarray(2.0 ** (-8.0 * np.arange(1, n + 1) / n), jnp.float32)


def cmp_to_sel_matrix(n_cmp, n_sel):
    cs = np.arange(n_cmp)[:, None] * CMP_STRIDE
    ss = np.arange(n_sel)[None, :] * SEL_BLOCK
    ov = np.clip(np.minimum(cs + CMP_BLOCK, ss + SEL_BLOCK) - np.maximum(cs, ss), 0, None) / CMP_BLOCK
    return jnp.asarray(ov, jnp.float32)


def s5_mixer(u, lam_re, lam_im, log_dt, b_re, b_im, c_re, c_im, d_skip, w_glu, b_glu):
    f32 = jnp.float32
    bsz, L, _ = u.shape
    uf = u.astype(f32)
    ug = uf.reshape(bsz, L, S5_GROUPS, S5_GROUP)
    lr = lam_re.astype(f32)
    li = lam_im.astype(f32)
    dt = jnp.exp(log_dt.astype(f32))[:, None]
    mag = jnp.exp(lr * dt)
    ab_re = mag * jnp.cos(li * dt)
    ab_im = mag * jnp.sin(li * dt)
    nr = ab_re - 1.0
    den = lr * lr + li * li
    f_re = (nr * lr + ab_im * li) / den
    f_im = (ab_im * lr - nr * li) / den
    bb_re = f_re[..., None] * b_re.astype(f32) - f_im[..., None] * b_im.astype(f32)
    bb_im = f_re[..., None] * b_im.astype(f32) + f_im[..., None] * b_re.astype(f32)
    bu_re = jnp.einsum('blgi,gpi->blgp', ug, bb_re)
    bu_im = jnp.einsum('blgi,gpi->blgp', ug, bb_im)
    a_re = jnp.broadcast_to(ab_re, bu_re.shape)
    a_im = jnp.broadcast_to(ab_im, bu_im.shape)

    def combine(e1, e2):
        a1r, a1i, b1r, b1i = e1
        a2r, a2i, b2r, b2i = e2
        return (a2r * a1r - a2i * a1i,
                a2r * a1i + a2i * a1r,
                a2r * b1r - a2i * b1i + b2r,
                a2r * b1i + a2i * b1r + b2i)

    _, _, x_re, x_im = lax.associative_scan(combine, (a_re, a_im, bu_re, bu_im), axis=1)
    y = (jnp.einsum('blgp,gip->blgi', x_re, c_re.astype(f32))
         - jnp.einsum('blgp,gip->blgi', x_im, c_im.astype(f32)))
    y = y.reshape(bsz, L, S5_CHANNELS) + d_skip.astype(f32) * uf
    y = jax.nn.gelu(y)
    y = y * jax.nn.sigmoid(y @ w_glu.astype(f32) + b_glu.astype(f32))
    return y.astype(u.dtype)


def stick_breaking_attention(q, k, v):
    L = q.shape[1]
    scale = SB_HEAD_DIM ** -0.5
    outs = []
    for blk in range(L // Q_BLOCK):
        t0 = blk * Q_BLOCK
        t1 = t0 + Q_BLOCK
        z = jnp.einsum('bqhd,bkhd->bhqk', q[:, t0:t1], k[:, :t1]).astype(jnp.float32) * scale
        tq = t0 + jnp.arange(Q_BLOCK)[:, None]
        sk = jnp.arange(t1)[None, :]
        mask = sk < tq
        log1m = jnp.where(mask, jax.nn.log_sigmoid(-z), 0.0)
        tail = lax.cumsum(log1m, axis=3, reverse=True) - log1m
        w = jnp.where(mask, jnp.exp(jax.nn.log_sigmoid(z) + tail), 0.0)
        outs.append(jnp.einsum('bhqk,bkhd->bqhd', w.astype(v.dtype), v[:, :t1]))
    return jnp.concatenate(outs, axis=1)


def even_mixer(x, w_in, lam_re, lam_im, log_dt, b_re, b_im, c_re, c_im, d_skip, w_glu, b_glu, w_out):
    bsz, L, _ = x.shape
    proj = _proj(x, w_in)
    u = proj[..., :S5_CHANNELS]
    o = S5_CHANNELS
    q = proj[..., o:o + SB_WIDTH].reshape(bsz, L, SB_HEADS, SB_HEAD_DIM)
    k = proj[..., o + SB_WIDTH:o + 2 * SB_WIDTH].reshape(bsz, L, SB_HEADS, SB_HEAD_DIM)
    v = proj[..., o + 2 * SB_WIDTH:].reshape(bsz, L, SB_HEADS, SB_HEAD_DIM)
    y_a = s5_mixer(u, lam_re, lam_im, log_dt, b_re, b_im, c_re, c_im, d_skip, w_glu, b_glu)
    y_b = stick_breaking_attention(q, k, v).reshape(bsz, L, SB_WIDTH).astype(x.dtype)
    return _proj(jnp.concatenate([y_a, y_b], axis=-1), w_out)


def nsa_mixer(x, w_in, pe_k, w1_k, w2_k, pe_v, w1_v, w2_v, w_out):
    f32 = jnp.float32
    bsz, L, _ = x.shape
    G, R, hd = NSA_KV_HEADS, NSA_GROUP, NSA_HEAD_DIM
    scale = hd ** -0.5
    proj = _proj(x, w_in)
    q = proj[..., :NSA_Q_WIDTH].reshape(bsz, L, G, R, hd)
    kv = proj[..., NSA_Q_WIDTH:NSA_Q_WIDTH + 2 * NSA_BRANCHES * NSA_KV_WIDTH].reshape(bsz, L, 2 * NSA_BRANCHES, G, hd)
    kc, vc, ks, vs, kw, vw = [kv[:, :, j] for j in range(2 * NSA_BRANCHES)]
    gates = jax.nn.sigmoid(proj[..., NSA_Q_WIDTH + 2 * NSA_BRANCHES * NSA_KV_WIDTH:].astype(f32))
    gates = gates.reshape(bsz, L, NSA_BRANCHES, G, R)
    slopes = alibi_slopes(NSA_HEADS).reshape(G, R)
    pos = jnp.arange(L)

    n_cmp = (L - CMP_BLOCK) // CMP_STRIDE + 1
    starts = jnp.arange(n_cmp) * CMP_STRIDE
    idx = starts[:, None] + jnp.arange(CMP_BLOCK)[None, :]

    def compress(t, pe, w1, w2):
        blocks = t[:, idx] + pe[None, None, :, None, :]
        blocks = jnp.moveaxis(blocks, 3, 2).reshape(bsz, n_cmp, G, CMP_BLOCK * hd)
        return jax.nn.gelu(blocks @ w1) @ w2

    k_cmp = compress(kc, pe_k, w1_k, w2_k)
    v_cmp = compress(vc, pe_v, w1_v, w2_v)
    ends = starts + CMP_BLOCK - 1
    dist_c = (pos[:, None] - ends[None, :]).astype(f32)
    s_c = jnp.einsum('btgrd,bngd->bgrtn', q, k_cmp).astype(f32) * scale
    s_c = s_c - slopes[:, :, None, None] * dist_c
    p_c = masked_softmax(s_c, dist_c >= 0)
    o_cmp = jnp.einsum('bgrtn,bngd->btgrd', p_c.astype(v_cmp.dtype), v_cmp)

    n_sel = L // SEL_BLOCK
    k_sel = min(SEL_TOPK, n_sel)
    imp = jnp.einsum('bgrtn,nj->bgtj', p_c, cmp_to_sel_matrix(n_cmp, n_sel))
    jb = jnp.arange(n_sel)[None, :]
    cur = (pos // SEL_BLOCK)[:, None]
    causal = jb * SEL_BLOCK <= pos[:, None]
    forced = (jb == 0) | (jb == cur) | (jb == cur - 1)
    imp = jnp.where(causal, imp + jnp.where(forced, FORCE_BONUS, 0.0), -jnp.inf)
    _, sel_idx = lax.top_k(imp, k_sel)

    ks_blk = jnp.moveaxis(ks.reshape(bsz, n_sel, SEL_BLOCK, G, hd), 3, 1)
    vs_blk = jnp.moveaxis(vs.reshape(bsz, n_sel, SEL_BLOCK, G, hd), 3, 1)
    gather_bg = jax.vmap(jax.vmap(lambda tbl, ix: tbl[ix]))
    n_chunk = L // SEL_Q_CHUNK
    q_ch = jnp.moveaxis(q.reshape(bsz, n_chunk, SEL_Q_CHUNK, G, R, hd), 1, 0)
    idx_ch = jnp.moveaxis(sel_idx.reshape(bsz, G, n_chunk, SEL_Q_CHUNK, k_sel), 2, 0)
    pos_ch = pos.reshape(n_chunk, SEL_Q_CHUNK)
    m_keys = k_sel * SEL_BLOCK

    def sel_chunk(args):
        qc, ic, tc = args
        kg = gather_bg(ks_blk, ic).reshape(bsz, G, SEL_Q_CHUNK, m_keys, hd)
        vg = gather_bg(vs_blk, ic).reshape(bsz, G, SEL_Q_CHUNK, m_keys, hd)
        kpos = (ic[..., None] * SEL_BLOCK + jnp.arange(SEL_BLOCK)).reshape(bsz, G, SEL_Q_CHUNK, m_keys)
        dist = (tc[None, None, :, None] - kpos).astype(f32)
        s = jnp.einsum('bcgrd,bgcmd->bgrcm', qc, kg).astype(f32) * scale
        s = s - slopes[None, :, :, None, None] * dist[:, :, None]
        p = masked_softmax(s, dist[:, :, None] >= 0)
        return jnp.einsum('bgrcm,bgcmd->bcgrd', p.astype(vg.dtype), vg)

    o_slc = lax.map(sel_chunk, (q_ch, idx_ch, pos_ch))
    o_slc = jnp.moveaxis(o_slc, 0, 1).reshape(bsz, L, G, R, hd)

    kw_pad = jnp.pad(kw, ((0, 0), (WINDOW, 0), (0, 0), (0, 0)))
    vw_pad = jnp.pad(vw, ((0, 0), (WINDOW, 0), (0, 0), (0, 0)))
    n_qb = L // Q_BLOCK
    span = WINDOW + Q_BLOCK
    q_b = jnp.moveaxis(q.reshape(bsz, n_qb, Q_BLOCK, G, R, hd), 1, 0)

    def win_block(args):
        qb, i = args
        t0 = i * Q_BLOCK
        kb = lax.dynamic_slice_in_dim(kw_pad, t0, span, axis=1)
        vb = lax.dynamic_slice_in_dim(vw_pad, t0, span, axis=1)
        tq = t0 + jnp.arange(Q_BLOCK)
        sk = t0 - WINDOW + jnp.arange(span)
        dist = tq[:, None] - sk[None, :]
        mask = (dist >= 0) & (dist < WINDOW) & (sk[None, :] >= 0)
        s = jnp.einsum('bqgrd,bkgd->bgrqk', qb, kb).astype(f32) * scale
        s = s - slopes[:, :, None, None] * dist.astype(f32)
        p = masked_softmax(s, mask)
        return jnp.einsum('bgrqk,bkgd->bqgrd', p.astype(vb.dtype), vb)

    o_win = lax.map(win_block, (q_b, jnp.arange(n_qb)))
    o_win = jnp.moveaxis(o_win, 0, 1).reshape(bsz, L, G, R, hd)

    o = (gates[:, :, 0, :, :, None] * o_cmp + gates[:, :, 1, :, :, None] * o_slc
         + gates[:, :, 2, :, :, None] * o_win)
    return _proj(o.reshape(bsz, L, NSA_Q_WIDTH).astype(x.dtype), w_out)


def peer_ffn(x, w_q, sub_keys, u_tab, v_tab):
    f32 = jnp.float32
    bsz, L, D = x.shape
    qall = _proj(x, w_q).reshape(-1, PEER_CHUNK, PEER_HEADS * PEER_KEY_DIM)
    tok = x.reshape(-1, PEER_CHUNK, D)

    def chunk(args):
        xc, q = args
        q = q.reshape(PEER_CHUNK, PEER_HEADS, 2, PEER_HALF)
        s1 = jnp.einsum('chd,hkd->chk', q[:, :, 0], sub_keys[:, 0]).astype(f32)
        s2 = jnp.einsum('chd,hkd->chk', q[:, :, 1], sub_keys[:, 1]).astype(f32)
        v1, i1 = lax.top_k(s1, PEER_TOPK)
        v2, i2 = lax.top_k(s2, PEER_TOPK)
        cand = (v1[..., :, None] + v2[..., None, :]).reshape(PEER_CHUNK, PEER_HEADS, PEER_TOPK * PEER_TOPK)
        top, flat = lax.top_k(cand, PEER_TOPK)
        e1 = jnp.take_along_axis(i1, flat // PEER_TOPK, axis=-1)
        e2 = jnp.take_along_axis(i2, flat % PEER_TOPK, axis=-1)
        expert = e1 * PEER_KEYS + e2
        g = jax.nn.softmax(top, axis=-1)
        h = jax.nn.gelu(jnp.einsum('cd,chkd->chk', xc, u_tab[expert]).astype(f32))
        return jnp.einsum('chk,chkd->cd', (g * h).astype(v_tab.dtype), v_tab[expert])

    return lax.map(chunk, (tok, qall)).reshape(bsz, L, D)


def kernel(x, even_w_in, even_lam_re, even_lam_im, even_log_dt, even_b_re, even_b_im, even_c_re, even_c_im, even_d, even_w_glu, even_b_glu, even_w_out, odd_w_in, odd_pe_k, odd_w1_k, odd_w2_k, odd_pe_v, odd_w1_v, odd_w2_v, odd_w_out, peer_w_q, peer_sub_keys, peer_u, peer_v, ln1_g, ln1_b, ln2_g, ln2_b):
    h = x
    for i in range(DEPTH):
        j = i // 2
        if i % 2 == 0:
            mix = even_mixer(h, even_w_in[j], even_lam_re[j], even_lam_im[j], even_log_dt[j],
                             even_b_re[j], even_b_im[j], even_c_re[j], even_c_im[j], even_d[j],
                             even_w_glu[j], even_b_glu[j], even_w_out[j])
        else:
            mix = nsa_mixer(h, odd_w_in[j], odd_pe_k[j], odd_w1_k[j], odd_w2_k[j],
                            odd_pe_v[j], odd_w1_v[j], odd_w2_v[j], odd_w_out[j])
        h = layer_norm(DN_ALPHA * h + mix, ln1_g[i], ln1_b[i])
        ffn = peer_ffn(h, peer_w_q[i], peer_sub_keys[i], peer_u[i], peer_v[i])
        h = layer_norm(DN_ALPHA * h + ffn, ln2_g[i], ln2_b[i])
    return h
```

```python
import functools
import math

import numpy as np
import jax
import jax.numpy as jnp
from jax import lax
from jax.experimental import pallas as pl
from jax.experimental.pallas import tpu as pltpu

D_MODEL = 1024
DEPTH = 2
S5_GROUP = 16
S5_CHANNELS = D_MODEL // 4
S5_GROUPS = S5_CHANNELS // S5_GROUP
S5_STATE = 64
SB_HEAD_DIM = 64
SB_HEADS = (D_MODEL - S5_CHANNELS) // SB_HEAD_DIM
SB_WIDTH = SB_HEADS * SB_HEAD_DIM
Q_BLOCK = 128
NSA_HEADS = 16
NSA_KV_HEADS = 4
NSA_GROUP = NSA_HEADS // NSA_KV_HEADS
NSA_HEAD_DIM = 64
NSA_Q_WIDTH = NSA_HEADS * NSA_HEAD_DIM
NSA_KV_WIDTH = NSA_KV_HEADS * NSA_HEAD_DIM
NSA_BRANCHES = 3
CMP_BLOCK = 32
CMP_STRIDE = 16
SEL_BLOCK = 64
SEL_TOPK = 16
SEL_Q_CHUNK = 16
WINDOW = 512
FORCE_BONUS = 1000.0
PEER_HEADS = 8
PEER_KEYS = 128
PEER_KEY_DIM = 256
PEER_HALF = PEER_KEY_DIM // 2
PEER_TOPK = 16
PEER_CHUNK = 128
DN_ALPHA = (2 * DEPTH) ** 0.25
LN_EPS = 1e-5

LANE = 128


def _mm_kernel(x_ref, w_ref, o_ref):
    o_ref[...] = jnp.dot(x_ref[...].astype(jnp.bfloat16), w_ref[...].astype(jnp.bfloat16),
                         preferred_element_type=jnp.float32)


def _matmul(x, w, tm=512, tn=512):
    m, k = x.shape
    n = w.shape[1]
    n_pad = -(-n // LANE) * LANE
    if n_pad != n:
        w = jnp.pad(w, ((0, 0), (0, n_pad - n)))
    while n_pad % tn:
        tn //= 2
    assert m % tm == 0 and tn >= LANE
    out = pl.pallas_call(
        _mm_kernel,
        grid=(m // tm, n_pad // tn),
        in_specs=[pl.BlockSpec((tm, k), lambda i, j: (i, 0)),
                  pl.BlockSpec((k, tn), lambda i, j: (0, j))],
        out_specs=pl.BlockSpec((tm, tn), lambda i, j: (i, j)),
        out_shape=jax.ShapeDtypeStruct((m, n_pad), jnp.float32),
        name="matmul",
    )(x, w)
    return out[:, :n] if n_pad != n else out


def _proj(x, w):
    b, l, d = x.shape
    return _matmul(x.reshape(b * l, d), w).reshape(b, l, w.shape[1])


def layer_norm(x, g, b):
    xf = x.astype(jnp.float32)
    mu = jnp.mean(xf, axis=-1, keepdims=True)
    var = jnp.mean(jnp.square(xf - mu), axis=-1, keepdims=True)
    return ((xf - mu) * lax.rsqrt(var + LN_EPS) * g + b).astype(x.dtype)


def s5_mixer(u, lam_re, lam_im, log_dt, b_re, b_im, c_re, c_im, d_skip, w_glu, b_glu):
    f32 = jnp.float32
    bsz, L, _ = u.shape
    uf = u.astype(f32)
    ug = uf.reshape(bsz, L, S5_GROUPS, S5_GROUP)
    lr = lam_re.astype(f32)
    li = lam_im.astype(f32)
    dt = jnp.exp(log_dt.astype(f32))[:, None]
    mag = jnp.exp(lr * dt)
    ab_re = mag * jnp.cos(li * dt)
    ab_im = mag * jnp.sin(li * dt)
    nr = ab_re - 1.0
    den = lr * lr + li * li
    f_re = (nr * lr + ab_im * li) / den
    f_im = (ab_im * lr - nr * li) / den
    bb_re = f_re[..., None] * b_re.astype(f32) - f_im[..., None] * b_im.astype(f32)
    bb_im = f_re[..., None] * b_im.astype(f32) + f_im[..., None] * b_re.astype(f32)
    bu_re = jnp.einsum('blgi,gpi->blgp', ug, bb_re)
    bu_im = jnp.einsum('blgi,gpi->blgp', ug, bb_im)
    a_re = jnp.broadcast_to(ab_re, bu_re.shape)
    a_im = jnp.broadcast_to(ab_im, bu_im.shape)

    def combine(e1, e2):
        a1r, a1i, b1r, b1i = e1
        a2r, a2i, b2r, b2i = e2
        return (a2r * a1r - a2i * a1i,
                a2r * a1i + a2i * a1r,
                a2r * b1r - a2i * b1i + b2r,
                a2r * b1i + a2i * b1r + b2i)

    _, _, x_re, x_im = lax.associative_scan(combine, (a_re, a_im, bu_re, bu_im), axis=1)
    y = (jnp.einsum('blgp,gip->blgi', x_re, c_re.astype(f32))
         - jnp.einsum('blgp,gip->blgi', x_im, c_im.astype(f32)))
    y = y.reshape(bsz, L, S5_CHANNELS) + d_skip.astype(f32) * uf
    y = jax.nn.gelu(y)
    y = y * jax.nn.sigmoid(y @ w_glu.astype(f32) + b_glu.astype(f32))
    return y.astype(u.dtype)


def stick_breaking_attention(q, k, v):
    L = q.shape[1]
    scale = SB_HEAD_DIM ** -0.5
    outs = []
    for blk in range(L // Q_BLOCK):
        t0 = blk * Q_BLOCK
        t1 = t0 + Q_BLOCK
        z = jnp.einsum('bqhd,bkhd->bhqk', q[:, t0:t1], k[:, :t1]).astype(jnp.float32) * scale
        tq = t0 + jnp.arange(Q_BLOCK)[:, None]
        sk = jnp.arange(t1)[None, :]
        mask = sk < tq
        log1m = jnp.where(mask, jax.nn.log_sigmoid(-z), 0.0)
        tail = lax.cumsum(log1m, axis=3, reverse=True) - log1m
        w = jnp.where(mask, jnp.exp(jax.nn.log_sigmoid(z) + tail), 0.0)
        outs.append(jnp.einsum('bhqk,bkhd->bqhd', w.astype(v.dtype), v[:, :t1]))
    return jnp.concatenate(outs, axis=1)


def even_mixer(x, w_in, lam_re, lam_im, log_dt, b_re, b_im, c_re, c_im, d_skip, w_glu, b_glu, w_out):
    bsz, L, _ = x.shape
    proj = _proj(x, w_in)
    u = proj[..., :S5_CHANNELS]
    o = S5_CHANNELS
    q = proj[..., o:o + SB_WIDTH].reshape(bsz, L, SB_HEADS, SB_HEAD_DIM)
    k = proj[..., o + SB_WIDTH:o + 2 * SB_WIDTH].reshape(bsz, L, SB_HEADS, SB_HEAD_DIM)
    v = proj[..., o + 2 * SB_WIDTH:].reshape(bsz, L, SB_HEADS, SB_HEAD_DIM)
    y_a = s5_mixer(u, lam_re, lam_im, log_dt, b_re, b_im, c_re, c_im, d_skip, w_glu, b_glu)
    y_b = stick_breaking_attention(q, k, v).reshape(bsz, L, SB_WIDTH).astype(x.dtype)
    return _proj(jnp.concatenate([y_a, y_b], axis=-1), w_out)


NEG_BIG = -1e30
NSA_TQ = 256
NSA_TK = 256
N_SEL = 32
N_CMP_PAD = 128
SEG = CMP_STRIDE


def _nsa_attn_kernel(q_ref, k_ref, v_ref, *rest, mode):
    if mode == "sel":
        sel_ref, o_ref = rest
    else:
        (o_ref,) = rest
    f32 = jnp.float32
    g = pl.program_id(1)
    i = pl.program_id(2)
    n_head, tq, hd = q_ref.shape[1:]
    tk = NSA_TK
    rows = n_head * tq
    t0 = i * tq
    q = q_ref[0].reshape(rows, hd)
    head = lax.broadcasted_iota(jnp.int32, (n_head, 1, 1), 0)
    slope = jnp.exp((g * n_head + head + 1).astype(f32) * (-8.0 / NSA_HEADS * math.log(2.0)))
    tpos = t0 + lax.broadcasted_iota(jnp.int32, (tq, tk), 0)
    scale = hd ** -0.5
    if mode == "sel":
        sel = sel_ref[0, 0]
        lo = 0
    else:
        lo = jnp.maximum(t0 - WINDOW, 0) // tk
    hi = (t0 + tq + tk - 1) // tk

    def chunk(kc, carry):
        m, l, acc = carry
        k0 = pl.multiple_of(kc * tk, tk)
        k = k_ref[0, 0, pl.ds(k0, tk), :]
        v = v_ref[0, 0, pl.ds(k0, tk), :]
        s = lax.dot_general(q, k, (((1,), (1,)), ((), ())), preferred_element_type=f32)
        dist = tpos - (k0 + lax.broadcasted_iota(jnp.int32, (tq, tk), 1))
        valid = dist >= 0
        if mode == "win":
            valid = valid & (dist < WINDOW)
        else:
            blk = lax.broadcasted_iota(jnp.int32, (N_SEL, tk), 0)
            kblk = (k0 + lax.broadcasted_iota(jnp.int32, (N_SEL, tk), 1)) // SEL_BLOCK
            picked = jnp.dot(sel, (blk == kblk).astype(f32), preferred_element_type=f32)
            valid = valid & (picked > 0.5)
        s = s.reshape(n_head, tq, tk) * scale - slope * dist.astype(f32)[None]
        s = jnp.where(valid[None], s, NEG_BIG)
        m_new = jnp.maximum(m, jnp.max(s, axis=2, keepdims=True))
        alpha = jnp.exp(m - m_new)
        p = jnp.where(valid[None], jnp.exp(s - m_new), 0.0)
        l = alpha * l + jnp.sum(p, axis=2, keepdims=True)
        pv = jnp.dot(p.reshape(rows, tk).astype(v.dtype), v, preferred_element_type=f32)
        acc = alpha * acc + pv.reshape(n_head, tq, hd)
        return m_new, l, acc

    init = (jnp.full((n_head, tq, 1), NEG_BIG, f32), jnp.zeros((n_head, tq, 1), f32),
            jnp.zeros((n_head, tq, hd), f32))
    m, l, acc = lax.fori_loop(lo, hi, chunk, init)
    o_ref[0] = acc / l


def _nsa_attn(q, k, v, sel, mode):
    b, h, L, hd = q.shape
    g = k.shape[1]
    r = h // g
    tq = NSA_TQ
    in_specs = [pl.BlockSpec((1, r, tq, hd), lambda bi, gi, i: (bi, gi, i, 0)),
                pl.BlockSpec((1, 1, L, hd), lambda bi, gi, i: (bi, gi, 0, 0)),
                pl.BlockSpec((1, 1, L, hd), lambda bi, gi, i: (bi, gi, 0, 0))]
    args = [q, k, v]
    if mode == "sel":
        in_specs.append(pl.BlockSpec((1, 1, tq, N_SEL), lambda bi, gi, i: (bi, gi, i, 0)))
        args.append(sel)
    return pl.pallas_call(
        functools.partial(_nsa_attn_kernel, mode=mode),
        grid=(b, g, L // tq),
        in_specs=in_specs,
        out_specs=pl.BlockSpec((1, r, tq, hd), lambda bi, gi, i: (bi, gi, i, 0)),
        out_shape=jax.ShapeDtypeStruct((b, h, L, hd), jnp.float32),
        compiler_params=pltpu.CompilerParams(dimension_semantics=("parallel", "parallel", "arbitrary"),
                                             vmem_limit_bytes=48 * 1024 * 1024),
        name="nsa_" + mode,
    )(*args)


def _nsa_cmp_kernel(q_ref, kseg_ref, vseg_ref, wk_ref, wv_ref, bk_ref, bv_ref, w2k_ref, w2v_ref,
                    o_ref, sel_ref, kc_scr, vc_scr):
    f32 = jnp.float32
    bf16 = jnp.bfloat16
    g = pl.program_id(1)
    n_head, L, hd = q_ref.shape[1:]
    n_cmp = N_CMP_PAD

    def compress(seg_ref, w_ref, b_ref, w2_ref):
        seg = seg_ref[0, 0]
        first = jnp.dot(seg, w_ref[0], preferred_element_type=f32)
        second = jnp.dot(seg, w_ref[1], preferred_element_type=f32)
        hid = first + pltpu.roll(second, n_cmp - 1, 0) + b_ref[...]
        return jnp.dot(jax.nn.gelu(hid).astype(bf16), w2_ref[...], preferred_element_type=f32)

    kc_scr[...] = compress(kseg_ref, wk_ref, bk_ref, w2k_ref).astype(bf16)
    vc_scr[...] = compress(vseg_ref, wv_ref, bv_ref, w2v_ref).astype(bf16)
    scale = hd ** -0.5
    tq = NSA_TQ
    rows = n_head * tq
    head = lax.broadcasted_iota(jnp.int32, (n_head, tq, 1), 0).reshape(rows, 1)
    slope = jnp.exp((g * n_head + head + 1).astype(f32) * (-8.0 / NSA_HEADS * math.log(2.0)))
    ends = lax.broadcasted_iota(jnp.int32, (1, n_cmp), 1) * CMP_STRIDE + (CMP_BLOCK - 1)
    cs = lax.broadcasted_iota(jnp.int32, (N_SEL, n_cmp), 1) * CMP_STRIDE
    ss = lax.broadcasted_iota(jnp.int32, (N_SEL, n_cmp), 0) * SEL_BLOCK
    overlap = jnp.maximum(jnp.minimum(cs + CMP_BLOCK, ss + SEL_BLOCK) - jnp.maximum(cs, ss), 0)
    overlap = overlap.astype(f32) * (1.0 / CMP_BLOCK)
    jrow = lax.broadcasted_iota(jnp.int32, (N_SEL, tq), 0)

    def qblock(i, carry):
        t0 = pl.multiple_of(i * tq, tq)
        q = q_ref[0, :, pl.ds(t0, tq), :].reshape(rows, hd)
        tpos = t0 + lax.broadcasted_iota(jnp.int32, (n_head, tq, 1), 1).reshape(rows, 1)
        dist = tpos - ends
        s = lax.dot_general(q, kc_scr[...], (((1,), (1,)), ((), ())), preferred_element_type=f32) * scale
        s = s - slope * dist.astype(f32)
        valid = dist >= 0
        s = jnp.where(valid, s, NEG_BIG)
        m = jnp.max(s, axis=1, keepdims=True)
        e = jnp.where(valid, jnp.exp(s - m), 0.0)
        p = e / jnp.maximum(jnp.sum(e, axis=1, keepdims=True), 1e-30)
        o = jnp.dot(p.astype(bf16), vc_scr[...], preferred_element_type=f32)
        o_ref[0, :, pl.ds(t0, tq), :] = o.reshape(n_head, tq, hd)
        psum = jnp.sum(p.reshape(n_head, tq, n_cmp), axis=0)
        imp = lax.dot_general(overlap, psum, (((1,), (1,)), ((), ())), preferred_element_type=f32,
                              precision=lax.Precision.HIGHEST)
        tcol = t0 + lax.broadcasted_iota(jnp.int32, (N_SEL, tq), 1)
        cur = tcol // SEL_BLOCK
        forced = (jrow == 0) | (jrow == cur) | (jrow == cur - 1)
        imp = jnp.where(jrow * SEL_BLOCK <= tcol, imp + jnp.where(forced, FORCE_BONUS, 0.0), -jnp.inf)
        rank = jnp.zeros((N_SEL, tq), jnp.int32)
        for j in range(N_SEL):
            other = imp[j:j + 1]
            ahead = (other > imp) | ((other == imp) & (j < jrow))
            rank = rank + ahead.astype(jnp.int32)
        sel_ref[0, 0, pl.ds(t0, tq), :] = (rank < SEL_TOPK).astype(f32).T
        return carry

    lax.fori_loop(0, L // tq, qblock, 0)


def _nsa_cmp(q, kseg, vseg, wk, wv, bk, bv, w2k, w2v):
    b, h, L, hd = q.shape
    g = kseg.shape[1]
    r = h // g
    full = lambda a: pl.BlockSpec(a.shape, lambda bi, gi: (0,) * a.ndim)
    seg_spec = pl.BlockSpec((1, 1) + kseg.shape[2:], lambda bi, gi: (bi, gi, 0, 0))
    return pl.pallas_call(
        _nsa_cmp_kernel,
        grid=(b, g),
        in_specs=[pl.BlockSpec((1, r, L, hd), lambda bi, gi: (bi, gi, 0, 0)), seg_spec, seg_spec,
                  full(wk), full(wv), full(bk), full(bv), full(w2k), full(w2v)],
        out_specs=[pl.BlockSpec((1, r, L, hd), lambda bi, gi: (bi, gi, 0, 0)),
                   pl.BlockSpec((1, 1, L, N_SEL), lambda bi, gi: (bi, gi, 0, 0))],
        out_shape=[jax.ShapeDtypeStruct((b, h, L, hd), jnp.float32),
                   jax.ShapeDtypeStruct((b, g, L, N_SEL), jnp.float32)],
        scratch_shapes=[pltpu.VMEM((N_CMP_PAD, hd), jnp.bfloat16), pltpu.VMEM((N_CMP_PAD, hd), jnp.bfloat16)],
        compiler_params=pltpu.CompilerParams(dimension_semantics=("parallel", "arbitrary"),
                                             vmem_limit_bytes=48 * 1024 * 1024),
        name="nsa_cmp",
    )(q, kseg, vseg, wk, wv, bk, bv, w2k, w2v)


def nsa_mixer(x, w_in, pe_k, w1_k, w2_k, pe_v, w1_v, w2_v, w_out):
    f32 = jnp.float32
    bf16 = jnp.bfloat16
    bsz, L, _ = x.shape
    G, R, hd = NSA_KV_HEADS, NSA_GROUP, NSA_HEAD_DIM
    proj = _proj(x, w_in)
    q = proj[..., :NSA_Q_WIDTH].reshape(bsz, L, NSA_HEADS, hd).transpose(0, 2, 1, 3).astype(bf16)
    kv = proj[..., NSA_Q_WIDTH:NSA_Q_WIDTH + 2 * NSA_BRANCHES * NSA_KV_WIDTH]
    kv = kv.reshape(bsz, L, 2 * NSA_BRANCHES, G, hd).transpose(2, 0, 3, 1, 4).astype(bf16)
    kc, vc, ks, vs, kw, vw = [kv[j] for j in range(2 * NSA_BRANCHES)]
    gates = jax.nn.sigmoid(proj[..., NSA_Q_WIDTH + 2 * NSA_BRANCHES * NSA_KV_WIDTH:].astype(f32))
    gates = gates.reshape(bsz, L, NSA_BRANCHES, G, R)

    def halves(w1):
        return w1.reshape(2, SEG * hd, -1).astype(bf16)

    def pos_term(pe, w1):
        return jnp.dot(pe.reshape(1, CMP_BLOCK * hd), w1, precision="highest")

    kseg = kc.reshape(bsz, G, L // SEG, SEG * hd)
    vseg = vc.reshape(bsz, G, L // SEG, SEG * hd)
    o_cmp, sel = _nsa_cmp(q, kseg, vseg, halves(w1_k), halves(w1_v), pos_term(pe_k, w1_k), pos_term(pe_v, w1_v),
                          w2_k.astype(bf16), w2_v.astype(bf16))
    o_slc = _nsa_attn(q, ks, vs, sel, "sel")
    o_win = _nsa_attn(q, kw, vw, None, "win")

    def heads_last(o):
        return o.transpose(0, 2, 1, 3).reshape(bsz, L, G, R, hd)

    o = (gates[:, :, 0, :, :, None] * heads_last(o_cmp) + gates[:, :, 1, :, :, None] * heads_last(o_slc)
         + gates[:, :, 2, :, :, None] * heads_last(o_win))
    return _proj(o.reshape(bsz, L, NSA_Q_WIDTH).astype(x.dtype), w_out)


PEER_TC = LANE
PEER_SEL = PEER_HEADS * PEER_TOPK
ROW_TILES = D_MODEL // LANE
HI_HALF = 0xFFFF0000
ROUTE_OFF_MASK = 0x7FFFFFFF
PEER_ROW_GROUP = 64
PEER_TOK_BATCH = 8
ROUTE_WORDS = PEER_TC * PEER_SEL


def _top16(s):
    n = s.shape[0]
    iota = lax.broadcasted_iota(jnp.int32, s.shape, 0)
    vals, idxs = [], []
    for _ in range(PEER_TOPK):
        m = jnp.max(s, axis=0, keepdims=True)
        idx = jnp.min(jnp.where(s == m, iota, n), axis=0, keepdims=True)
        vals.append(m)
        idxs.append(idx)
        s = jnp.where(iota == idx, -jnp.inf, s)
    return jnp.concatenate(vals, axis=0), jnp.concatenate(idxs, axis=0)


def _peer_topk_kernel(x_ref, wq_ref, keys_ref, code_ref, g_ref):
    f32 = jnp.float32
    t = x_ref.shape[0]
    xb = x_ref[...].astype(jnp.bfloat16)
    j8 = lax.broadcasted_iota(jnp.int32, (8, t), 0)
    j16 = lax.broadcasted_iota(jnp.int32, (16, t), 0)
    neg = -jnp.inf
    experts, gates = [], []
    for h in range(PEER_HEADS):
        q = jnp.dot(xb, wq_ref[:, h * PEER_KEY_DIM:(h + 1) * PEER_KEY_DIM],
                    preferred_element_type=f32).astype(jnp.bfloat16)
        dn = (((1,), (1,)), ((), ()))
        s1 = lax.dot_general(keys_ref[2 * h], q[:, :PEER_HALF], dn, preferred_element_type=f32)
        s2 = lax.dot_general(keys_ref[2 * h + 1], q[:, PEER_HALF:], dn, preferred_element_type=f32)
        v1, i1 = _top16(s1)
        v2, i2 = _top16(s2)
        cand = [v1[0:1] + v2]
        flat = [j16]
        pay = [i1[0:1] * PEER_KEYS + i2]
        for i in range(1, 8):
            keep = j8 < PEER_TOPK // (i + 1)
            cand.append(jnp.where(keep, v1[i:i + 1] + v2[0:8], neg))
            flat.append(i * PEER_TOPK + j8)
            pay.append(i1[i:i + 1] * PEER_KEYS + i2[0:8])
        cand.append(v1[8:16] + v2[0:1])
        flat.append((8 + j8) * PEER_TOPK)
        pay.append(i1[8:16] * PEER_KEYS + i2[0:1])
        cand = jnp.concatenate(cand, axis=0)
        flat = jnp.concatenate(flat, axis=0)
        pay = jnp.concatenate(pay, axis=0)
        tops = []
        for _ in range(PEER_TOPK):
            m = jnp.max(cand, axis=0, keepdims=True)
            fl = jnp.min(jnp.where(cand == m, flat, PEER_TOPK * PEER_TOPK), axis=0, keepdims=True)
            sel = flat == fl
            experts.append(jnp.max(jnp.where(sel, pay, -1), axis=0, keepdims=True))
            tops.append(m)
            cand = jnp.where(sel, neg, cand)
        top = jnp.concatenate(tops, axis=0)
        e = jnp.exp(top - top[0:1])
        gates.append(e / jnp.sum(e, axis=0, keepdims=True))
    expert = jnp.concatenate(experts, axis=0)
    code = ((expert >> 1) * ROW_TILES) | (((expert & 1) ^ 1) << 31)
    code_ref[0] = code.T
    g_ref[0] = jnp.concatenate(gates, axis=0)


def _peer_topk(x2, wq_b, keys_b):
    n_tok, d = x2.shape
    n_chunk = n_tok // PEER_TC
    return pl.pallas_call(
        _peer_topk_kernel,
        grid=(n_chunk,),
        in_specs=[pl.BlockSpec((PEER_TC, d), lambda i: (i, 0)),
                  pl.BlockSpec(wq_b.shape, lambda i: (0, 0)),
                  pl.BlockSpec(keys_b.shape, lambda i: (0, 0, 0))],
        out_specs=[pl.BlockSpec((1, PEER_TC, PEER_SEL), lambda i: (i, 0, 0)),
                   pl.BlockSpec((1, PEER_SEL, PEER_TC), lambda i: (i, 0, 0))],
        out_shape=[jax.ShapeDtypeStruct((n_chunk, PEER_TC, PEER_SEL), jnp.int32),
                   jax.ShapeDtypeStruct((n_chunk, PEER_SEL, PEER_TC), jnp.float32)],
        compiler_params=pltpu.CompilerParams(dimension_semantics=("arbitrary",),
                                             vmem_limit_bytes=48 * 1024 * 1024),
        name="peer_topk",
    )(x2, wq_b, keys_b)


def _pack_table(tab):
    n, d = tab.shape
    b = lax.bitcast_convert_type(tab.astype(jnp.bfloat16), jnp.uint16).astype(jnp.uint32)
    b = b.reshape(n // 2, 2, d)
    return (b[:, 0] | (b[:, 1] << 16)).reshape(n // 2 * (d // LANE), LANE)


def _table_row(tab_ref, code):
    off = pl.multiple_of(code & ROUTE_OFF_MASK, ROW_TILES)
    t = tab_ref[pl.ds(off, ROW_TILES), :]
    sh = jnp.where(jnp.full(t.shape, code, jnp.int32) < 0, jnp.uint32(16), jnp.uint32(0))
    return pltpu.bitcast((t << sh) & jnp.uint32(HI_HALF), jnp.float32)


def _rowsum8(ps):
    sub = lax.broadcasted_iota(jnp.int32, (8, LANE), 0)
    for d in (1, 2, 4):
        keep = (sub & (2 * d - 1)) < d
        ps = [jnp.where(keep, a, b) + pltpu.roll(jnp.where(keep, b, a), d, 0)
              for a, b in zip(ps[0::2], ps[1::2])]
    return ps[0]


def _peer_prefetch(i, n, hbm_refs, smem_refs, sem, tab_hbm, tab_vmem, sem_tab):
    slot = i % 2

    def copies(chunk, s):
        return [pltpu.make_async_copy(h.at[pl.ds(chunk * ROUTE_WORDS, ROUTE_WORDS)],
                                      m.at[pl.ds(s * ROUTE_WORDS, ROUTE_WORDS)], sem.at[k, s])
                for k, (h, m) in enumerate(zip(hbm_refs, smem_refs))]

    @pl.when(i == 0)
    def _():
        tab = pltpu.make_async_copy(tab_hbm, tab_vmem, sem_tab)
        tab.start()
        for c in copies(0, 0):
            c.start()
        tab.wait()

    @pl.when(i + 1 < n)
    def _():
        for c in copies(i + 1, 1 - slot):
            c.start()

    for c in copies(i, slot):
        c.wait()
    return slot


def _peer_u_kernel(code_hbm, tab_hbm, x_ref, g_ref, w_ref, tab_vmem, code_s, r_scr, sem, sem_tab):
    i = pl.program_id(0)
    slot = _peer_prefetch(i, pl.num_programs(0), (code_hbm,), (code_s,), sem, tab_hbm, tab_vmem, sem_tab)
    t = x_ref.shape[0]
    lane = lax.broadcasted_iota(jnp.int32, (PEER_SEL, t), 1)

    def token_batch(cb, hacc):
        for j in range(PEER_TOK_BATCH):
            c = cb * PEER_TOK_BATCH + j
            xt = x_ref[c]

            def group(gi, carry, c=c, xt=xt, j=j):
                base = pl.multiple_of(gi * PEER_ROW_GROUP, PEER_ROW_GROUP)
                word = (slot * t + c) * PEER_SEL + base
                sums = [_rowsum8([_table_row(tab_vmem, code_s[word + b * 8 + r]) * xt for r in range(8)])
                        for b in range(PEER_ROW_GROUP // 8)]
                r_scr[pl.ds(j * PEER_SEL + base, PEER_ROW_GROUP), :] = jnp.concatenate(sums, axis=0)
                return carry

            lax.fori_loop(0, PEER_SEL // PEER_ROW_GROUP, group, 0)
        col = jnp.sum(r_scr[...], axis=1, keepdims=True)
        for j in range(PEER_TOK_BATCH):
            hacc = jnp.where(lane == cb * PEER_TOK_BATCH + j, col[j * PEER_SEL:(j + 1) * PEER_SEL], hacc)
        return hacc

    hacc = lax.fori_loop(0, t // PEER_TOK_BATCH, token_batch, jnp.zeros((PEER_SEL, t), jnp.float32))
    w_ref[0] = (g_ref[0] * jax.nn.gelu(hacc)).T


def _peer_v_kernel(code_hbm, w_hbm, tab_hbm, o_ref, tab_vmem, code_s, w_s, sem, sem_tab):
    i = pl.program_id(0)
    slot = _peer_prefetch(i, pl.num_programs(0), (code_hbm, w_hbm), (code_s, w_s), sem,
                          tab_hbm, tab_vmem, sem_tab)
    n_acc = 4
    zero = jnp.zeros((ROW_TILES, LANE), jnp.float32)

    def token(c, carry):
        def group(gi, accs):
            word = (slot * PEER_TC + c) * PEER_SEL + gi * PEER_ROW_GROUP
            accs = list(accs)
            for r in range(PEER_ROW_GROUP):
                term = _table_row(tab_vmem, code_s[word + r]) * w_s[word + r]
                accs[r % n_acc] = accs[r % n_acc] + term
            return tuple(accs)

        accs = lax.fori_loop(0, PEER_SEL // PEER_ROW_GROUP, group, (zero,) * n_acc)
        o_ref[c] = (accs[0] + accs[1]) + (accs[2] + accs[3])
        return carry

    lax.fori_loop(0, o_ref.shape[0], token, 0)


def _peer_gather_params():
    return pltpu.CompilerParams(dimension_semantics=("arbitrary",), vmem_limit_bytes=56 * 1024 * 1024)


def _peer_u(code, tab_u, x3, g):
    n_chunk = code.shape[0] // ROUTE_WORDS
    any_spec = pl.BlockSpec(memory_space=pl.ANY)
    return pl.pallas_call(
        _peer_u_kernel,
        grid=(n_chunk,),
        in_specs=[any_spec, any_spec,
                  pl.BlockSpec((PEER_TC, ROW_TILES, LANE), lambda i: (i, 0, 0)),
                  pl.BlockSpec((1, PEER_SEL, PEER_TC), lambda i: (i, 0, 0))],
        out_specs=pl.BlockSpec((1, PEER_TC, PEER_SEL), lambda i: (i, 0, 0)),
        out_shape=jax.ShapeDtypeStruct((n_chunk, PEER_TC, PEER_SEL), jnp.float32),
        scratch_shapes=[pltpu.VMEM(tab_u.shape, jnp.uint32),
                        pltpu.SMEM((2 * ROUTE_WORDS,), jnp.int32),
                        pltpu.VMEM((PEER_TOK_BATCH * PEER_SEL, LANE), jnp.float32),
                        pltpu.SemaphoreType.DMA((1, 2)), pltpu.SemaphoreType.DMA(())],
        compiler_params=_peer_gather_params(),
        name="peer_u",
    )(code, tab_u, x3, g)


def _peer_v(code, w, tab_v):
    n_chunk = code.shape[0] // ROUTE_WORDS
    any_spec = pl.BlockSpec(memory_space=pl.ANY)
    return pl.pallas_call(
        _peer_v_kernel,
        grid=(n_chunk,),
        in_specs=[any_spec, any_spec, any_spec],
        out_specs=pl.BlockSpec((PEER_TC, ROW_TILES, LANE), lambda i: (i, 0, 0)),
        out_shape=jax.ShapeDtypeStruct((n_chunk * PEER_TC, ROW_TILES, LANE), jnp.float32),
        scratch_shapes=[pltpu.VMEM(tab_v.shape, jnp.uint32),
                        pltpu.SMEM((2 * ROUTE_WORDS,), jnp.int32),
                        pltpu.SMEM((2 * ROUTE_WORDS,), jnp.float32),
                        pltpu.SemaphoreType.DMA((2, 2)), pltpu.SemaphoreType.DMA(())],
        compiler_params=_peer_gather_params(),
        name="peer_v",
    )(code, w, tab_v)


def peer_ffn(x, w_q, sub_keys, u_tab, v_tab):
    bsz, L, D = x.shape
    x2 = x.reshape(bsz * L, D)
    keys_b = sub_keys.reshape(PEER_HEADS * 2, PEER_KEYS, PEER_HALF).astype(jnp.bfloat16)
    code, g = _peer_topk(x2, w_q.astype(jnp.bfloat16), keys_b)
    code = code.reshape(-1)
    w = _peer_u(code, _pack_table(u_tab), x2.reshape(bsz * L, ROW_TILES, LANE), g)
    out = _peer_v(code, w.reshape(-1), _pack_table(v_tab))
    return out.reshape(bsz, L, D)


def kernel(x, even_w_in, even_lam_re, even_lam_im, even_log_dt, even_b_re, even_b_im, even_c_re, even_c_im, even_d, even_w_glu, even_b_glu, even_w_out, odd_w_in, odd_pe_k, odd_w1_k, odd_w2_k, odd_pe_v, odd_w1_v, odd_w2_v, odd_w_out, peer_w_q, peer_sub_keys, peer_u, peer_v, ln1_g, ln1_b, ln2_g, ln2_b):
    h = x
    for i in range(DEPTH):
        j = i // 2
        if i % 2 == 0:
            mix = even_mixer(h, even_w_in[j], even_lam_re[j], even_lam_im[j], even_log_dt[j],
                             even_b_re[j], even_b_im[j], even_c_re[j], even_c_im[j], even_d[j],
                             even_w_glu[j], even_b_glu[j], even_w_out[j])
        else:
            mix = nsa_mixer(h, odd_w_in[j], odd_pe_k[j], odd_w1_k[j], odd_w2_k[j],
                            odd_pe_v[j], odd_w1_v[j], odd_w2_v[j], odd_w_out[j])
        h = layer_norm(DN_ALPHA * h + mix, ln1_g[i], ln1_b[i])
        ffn = peer_ffn(h, peer_w_q[i], peer_sub_keys[i], peer_u[i], peer_v[i])
        h = layer_norm(DN_ALPHA * h + ffn, ln2_g[i], ln2_b[i])
    return h
```

```python
import functools
import math

import numpy as np
import jax
import jax.numpy as jnp
from jax import lax
from jax.experimental import pallas as pl
from jax.experimental.pallas import tpu as pltpu

D_MODEL = 1024
DEPTH = 2
S5_GROUP = 16
S5_CHANNELS = D_MODEL // 4
S5_GROUPS = S5_CHANNELS // S5_GROUP
S5_STATE = 64
SB_HEAD_DIM = 64
SB_HEADS = (D_MODEL - S5_CHANNELS) // SB_HEAD_DIM
SB_WIDTH = SB_HEADS * SB_HEAD_DIM
NSA_HEADS = 16
NSA_KV_HEADS = 4
NSA_GROUP = NSA_HEADS // NSA_KV_HEADS
NSA_HEAD_DIM = 64
NSA_Q_WIDTH = NSA_HEADS * NSA_HEAD_DIM
NSA_KV_WIDTH = NSA_KV_HEADS * NSA_HEAD_DIM
NSA_BRANCHES = 3
CMP_BLOCK = 32
CMP_STRIDE = 16
SEL_BLOCK = 64
SEL_TOPK = 16
WINDOW = 512
FORCE_BONUS = 1000.0
PEER_HEADS = 8
PEER_KEYS = 128
PEER_KEY_DIM = 256
PEER_HALF = PEER_KEY_DIM // 2
PEER_TOPK = 16
DN_ALPHA = (2 * DEPTH) ** 0.25
LN_EPS = 1e-5

LANE = 128
VMEM_LIMIT = 48 * 1024 * 1024


def _mm_kernel(x_ref, w_ref, o_ref):
    o_ref[...] = jnp.dot(x_ref[...].astype(jnp.bfloat16), w_ref[...].astype(jnp.bfloat16),
                         preferred_element_type=jnp.float32)


def _matmul(x, w, tm=512, tn=512):
    m, k = x.shape
    n = w.shape[1]
    n_pad = -(-n // LANE) * LANE
    if n_pad != n:
        w = jnp.pad(w, ((0, 0), (0, n_pad - n)))
    while n_pad % tn:
        tn //= 2
    assert m % tm == 0 and tn >= LANE
    out = pl.pallas_call(
        _mm_kernel,
        grid=(m // tm, n_pad // tn),
        in_specs=[pl.BlockSpec((tm, k), lambda i, j: (i, 0)),
                  pl.BlockSpec((k, tn), lambda i, j: (0, j))],
        out_specs=pl.BlockSpec((tm, tn), lambda i, j: (i, j)),
        out_shape=jax.ShapeDtypeStruct((m, n_pad), jnp.float32),
        name="matmul",
    )(x, w)
    return out[:, :n] if n_pad != n else out


def _ln_rows(y, g, b):
    mu = jnp.mean(y, axis=-1, keepdims=True)
    d = y - mu
    var = jnp.mean(d * d, axis=-1, keepdims=True)
    return d * lax.rsqrt(var + LN_EPS) * g + b


def _res_ln_kernel(h_ref, m_ref, g_ref, b_ref, o_ref):
    o_ref[...] = _ln_rows(DN_ALPHA * h_ref[...] + m_ref[...], g_ref[...], b_ref[...])


def _res_ln(h2, mix2, g, b, tm=512):
    m, d = h2.shape
    row = pl.BlockSpec((tm, d), lambda i: (i, 0))
    vec = pl.BlockSpec((1, d), lambda i: (0, 0))
    return pl.pallas_call(
        _res_ln_kernel,
        grid=(m // tm,),
        in_specs=[row, row, vec, vec],
        out_specs=row,
        out_shape=jax.ShapeDtypeStruct((m, d), jnp.float32),
        compiler_params=pltpu.CompilerParams(dimension_semantics=("parallel",), vmem_limit_bytes=VMEM_LIMIT),
        name="res_ln",
    )(h2, mix2, g.reshape(1, d), b.reshape(1, d))


def _proj_res_ln_kernel(*refs, n_sum):
    n_x = sum(n_sum)
    xs, ws = refs[:n_x], refs[n_x:n_x + len(n_sum)]
    h_ref, g_ref, b_ref, o_ref = refs[n_x + len(n_sum):]
    acc = DN_ALPHA * h_ref[...]
    k = 0
    for cnt, w_ref in zip(n_sum, ws):
        x = xs[k][...]
        for j in range(1, cnt):
            x = x + xs[k + j][...]
        k += cnt
        acc = acc + jnp.dot(x.astype(jnp.bfloat16), w_ref[...], preferred_element_type=jnp.float32)
    o_ref[...] = _ln_rows(acc, g_ref[...], b_ref[...])


def _proj_res_ln(groups, h2, g, b, tm=512):
    m, d = h2.shape
    xs, ws, n_sum, in_specs = [], [], [], []
    for arrs, w in groups:
        n_sum.append(len(arrs))
        for a in arrs:
            xs.append(a)
            in_specs.append(pl.BlockSpec((tm, a.shape[1]), lambda i: (i, 0)))
    for _, w in groups:
        ws.append(w)
        in_specs.append(pl.BlockSpec(w.shape, lambda i: (0, 0)))
    row = pl.BlockSpec((tm, d), lambda i: (i, 0))
    vec = pl.BlockSpec((1, d), lambda i: (0, 0))
    return pl.pallas_call(
        functools.partial(_proj_res_ln_kernel, n_sum=tuple(n_sum)),
        grid=(m // tm,),
        in_specs=in_specs + [row, vec, vec],
        out_specs=row,
        out_shape=jax.ShapeDtypeStruct((m, d), jnp.float32),
        compiler_params=pltpu.CompilerParams(dimension_semantics=("parallel",), vmem_limit_bytes=VMEM_LIMIT),
        name="proj_res_ln",
    )(*xs, *ws, h2, g.reshape(1, d), b.reshape(1, d))


S5_T = 64
S5_FLAT = S5_GROUP * S5_T
S5_REAL = 2 * S5_STATE


def _s5_operators(lam_re, lam_im, log_dt, b_re, b_im, c_re, c_im):
    f32 = jnp.float32
    T = S5_T
    lr = lam_re.astype(f32)
    li = lam_im.astype(f32)
    dt = jnp.exp(log_dt.astype(f32))[:, None]
    mag = jnp.exp(lr * dt)
    ab_re = mag * jnp.cos(li * dt)
    ab_im = mag * jnp.sin(li * dt)
    nr = ab_re - 1.0
    den = lr * lr + li * li
    f_re = (nr * lr + ab_im * li) / den
    f_im = (ab_im * lr - nr * li) / den
    bb_re = f_re[..., None] * b_re.astype(f32) - f_im[..., None] * b_im.astype(f32)
    bb_im = f_re[..., None] * b_im.astype(f32) + f_im[..., None] * b_re.astype(f32)
    cr = c_re.astype(f32)
    ci = c_im.astype(f32)
    n = jnp.arange(T + 1, dtype=f32)[None, :, None]
    pw_mag = jnp.exp(n * (lr * dt)[:, None, :])
    pw_re = pw_mag * jnp.cos(n * (li * dt)[:, None, :])
    pw_im = pw_mag * jnp.sin(n * (li * dt)[:, None, :])
    hp = "highest"
    cb_re = cr[:, :, :, None] * bb_re[:, None] - ci[:, :, :, None] * bb_im[:, None]
    cb_im = cr[:, :, :, None] * bb_im[:, None] + ci[:, :, :, None] * bb_re[:, None]
    K = (jnp.einsum('gtp,gipj->gtij', pw_re[:, :T], cb_re, precision=hp)
         - jnp.einsum('gtp,gipj->gtij', pw_im[:, :T], cb_im, precision=hp))
    s_idx = jnp.arange(T)[:, None]
    t_idx = jnp.arange(T)[None, :]
    lag = t_idx - s_idx
    Kst = jnp.where((lag >= 0)[None, :, :, None, None], K[:, jnp.clip(lag, 0, T - 1)], 0.0)
    M = Kst.transpose(0, 4, 1, 3, 2).reshape(-1, S5_FLAT, S5_FLAT)
    rev_re = pw_re[:, :T][:, ::-1]
    rev_im = pw_im[:, :T][:, ::-1]
    bt_re = bb_re.transpose(0, 2, 1)[:, :, None]
    bt_im = bb_im.transpose(0, 2, 1)[:, :, None]
    F_re = rev_re[:, None] * bt_re - rev_im[:, None] * bt_im
    F_im = rev_re[:, None] * bt_im + rev_im[:, None] * bt_re
    F = jnp.concatenate([F_re, F_im], axis=-1).reshape(-1, S5_FLAT, S5_REAL)
    up_re = pw_re[:, 1:]
    up_im = pw_im[:, 1:]
    E_re = cr[:, :, None] * up_re[:, None] - ci[:, :, None] * up_im[:, None]
    E_im = -(cr[:, :, None] * up_im[:, None] + ci[:, :, None] * up_re[:, None])
    E = jnp.concatenate([E_re, E_im], axis=-1).reshape(-1, S5_FLAT, S5_REAL).transpose(0, 2, 1)
    AT = jnp.stack([jnp.concatenate([pw_re[:, T], pw_re[:, T]], axis=-1),
                    jnp.concatenate([-pw_im[:, T], pw_im[:, T]], axis=-1)], axis=1)
    bf16 = jnp.bfloat16
    return M.astype(bf16), F.astype(bf16), E.astype(bf16), AT


def _s5_kernel(u_ref, m_ref, f_ref, e_ref, at_ref, y_ref, sin_scr, *, n_chunk, bsz):
    f32 = jnp.float32
    u = u_ref[0]
    sloc = jnp.dot(u, f_ref[0], preferred_element_type=f32)
    a_main = at_ref[0, 0:1]
    a_cross = at_ref[0, 1:2]
    carry = jnp.zeros((bsz, S5_REAL), f32)
    for c in range(n_chunk):
        sin_scr[c * bsz:(c + 1) * bsz, :] = carry
        carry = carry * a_main + pltpu.roll(carry, S5_STATE, 1) * a_cross + sloc[c * bsz:(c + 1) * bsz]
    y = jnp.dot(u, m_ref[0], preferred_element_type=f32)
    y_ref[0] = y + jnp.dot(sin_scr[...].astype(jnp.bfloat16), e_ref[0], preferred_element_type=f32)


def _s5_scan(u, ops):
    M, F, E, AT = ops
    bsz, L, _ = u.shape
    n_chunk = L // S5_T
    rows = n_chunk * bsz
    uf = u.reshape(bsz, n_chunk, S5_T, S5_GROUPS, S5_GROUP).transpose(3, 1, 0, 4, 2)
    uf = uf.reshape(S5_GROUPS, rows, S5_FLAT).astype(jnp.bfloat16)
    grp = lambda a: pl.BlockSpec((1,) + a.shape[1:], lambda g: (g,) + (0,) * (a.ndim - 1))
    yf = pl.pallas_call(
        functools.partial(_s5_kernel, n_chunk=n_chunk, bsz=bsz),
        grid=(S5_GROUPS,),
        in_specs=[grp(uf), grp(M), grp(F), grp(E), grp(AT)],
        out_specs=pl.BlockSpec((1, rows, S5_FLAT), lambda g: (g, 0, 0)),
        out_shape=jax.ShapeDtypeStruct((S5_GROUPS, rows, S5_FLAT), jnp.float32),
        scratch_shapes=[pltpu.VMEM((rows, S5_REAL), jnp.float32)],
        compiler_params=pltpu.CompilerParams(dimension_semantics=("parallel",), vmem_limit_bytes=VMEM_LIMIT),
        name="s5_scan",
    )(uf, M, F, E, AT)
    yf = yf.reshape(S5_GROUPS, n_chunk, bsz, S5_GROUP, S5_T).transpose(2, 1, 4, 0, 3)
    return yf.reshape(bsz, L, S5_CHANNELS)


def _s5_post_kernel(y_ref, u_ref, d_ref, wg_ref, bg_ref, o_ref):
    z = jax.nn.gelu(y_ref[...] + d_ref[...] * u_ref[...])
    gate = jnp.dot(z.astype(jnp.bfloat16), wg_ref[...], preferred_element_type=jnp.float32) + bg_ref[...]
    o_ref[...] = z * jax.nn.sigmoid(gate)


def _s5_post(y2, proj2, d_skip, w_glu, b_glu, tm=512):
    m = y2.shape[0]
    c = S5_CHANNELS
    row = pl.BlockSpec((tm, c), lambda i: (i, 0))
    vec = pl.BlockSpec((1, c), lambda i: (0, 0))
    return pl.pallas_call(
        _s5_post_kernel,
        grid=(m // tm,),
        in_specs=[row, row, vec, pl.BlockSpec((c, c), lambda i: (0, 0)), vec],
        out_specs=row,
        out_shape=jax.ShapeDtypeStruct((m, c), jnp.float32),
        name="s5_post",
    )(y2, proj2, d_skip.reshape(1, c), w_glu.astype(jnp.bfloat16), b_glu.reshape(1, c))


SB_TQ = 256
SB_PAIR = LANE // SB_HEAD_DIM


def _sb_kernel(q_ref, k_ref, v_ref, o_ref):
    f32 = jnp.float32
    bf16 = jnp.bfloat16
    i = pl.program_id(2)
    n_head, tq, hd = q_ref.shape[1:]
    tk = tq
    scale = hd ** -0.5
    t0 = i * tq
    tpos = t0 + lax.broadcasted_iota(jnp.int32, (tq, tk), 0)
    col = lax.broadcasted_iota(jnp.int32, (tk, tk), 1)
    row = lax.broadcasted_iota(jnp.int32, (tk, tk), 0)
    later = (row > col).astype(bf16)
    qs = [q_ref[0, hh] for hh in range(n_head)]

    def chunk(step, carry):
        k0 = pl.multiple_of((i - step) * tk, tk)
        mask = (k0 + lax.broadcasted_iota(jnp.int32, (tq, tk), 1)) < tpos
        out = []
        for hh in range(n_head):
            run, acc = carry[hh]
            k = k_ref[0, hh, pl.ds(k0, tk), :]
            v = v_ref[0, hh, pl.ds(k0, tk), :]
            z = lax.dot_general(qs[hh], k, (((1,), (1,)), ((), ())), preferred_element_type=f32) * scale
            log_sig = jnp.minimum(z, 0.0) - jnp.log(1.0 + jnp.exp(-jnp.abs(z)))
            log1m = jnp.where(mask, log_sig - z, 0.0)
            hi = log1m.astype(bf16)
            lo = (log1m - hi.astype(f32)).astype(bf16)
            tail = jnp.dot(hi, later, preferred_element_type=f32) + jnp.dot(lo, later, preferred_element_type=f32)
            w = jnp.where(mask, jnp.exp(log_sig + tail + run), 0.0)
            acc = acc + jnp.dot(w.astype(bf16), v, preferred_element_type=f32)
            run = run + jnp.sum(log1m, axis=1, keepdims=True)
            out.append((run, acc))
        return tuple(out)

    init = tuple((jnp.zeros((tq, 1), f32), jnp.zeros((tq, hd), f32)) for _ in range(n_head))
    res = lax.fori_loop(0, i + 1, chunk, init)
    o_ref[0] = jnp.concatenate([acc for _, acc in res], axis=1)


def _sb_attention(q, k, v):
    b, h, L, hd = q.shape
    tq = SB_TQ
    kv_spec = pl.BlockSpec((1, SB_PAIR, L, hd), lambda bi, hi, i: (bi, hi, 0, 0))
    return pl.pallas_call(
        _sb_kernel,
        grid=(b, h // SB_PAIR, L // tq),
        in_specs=[pl.BlockSpec((1, SB_PAIR, tq, hd), lambda bi, hi, i: (bi, hi, i, 0)), kv_spec, kv_spec],
        out_specs=pl.BlockSpec((1, tq, SB_PAIR * hd), lambda bi, hi, i: (bi, i, hi)),
        out_shape=jax.ShapeDtypeStruct((b, L, h * hd), jnp.float32),
        compiler_params=pltpu.CompilerParams(dimension_semantics=("parallel", "parallel", "arbitrary"),
                                             vmem_limit_bytes=VMEM_LIMIT),
        name="sb_attn",
    )(q, k, v)


def _heads_first(a, n_head):
    b, L, w = a.shape
    return a.reshape(b, L, n_head, w // n_head).transpose(0, 2, 1, 3).astype(jnp.bfloat16)


def even_mixer(x, w_in, lam_re, lam_im, log_dt, b_re, b_im, c_re, c_im, d_skip, w_glu, b_glu, w_out,
               ln_g, ln_b):
    bsz, L, D = x.shape
    x2 = x.reshape(bsz * L, D)
    proj = _matmul(x2, w_in)
    p3 = proj.reshape(bsz, L, -1)
    o = S5_CHANNELS
    y = _s5_scan(p3[..., :o], _s5_operators(lam_re, lam_im, log_dt, b_re, b_im, c_re, c_im))
    y_a = _s5_post(y.reshape(bsz * L, o), proj, d_skip, w_glu, b_glu)
    q = _heads_first(p3[..., o:o + SB_WIDTH], SB_HEADS)
    k = _heads_first(p3[..., o + SB_WIDTH:o + 2 * SB_WIDTH], SB_HEADS)
    v = _heads_first(p3[..., o + 2 * SB_WIDTH:], SB_HEADS)
    y_b = _sb_attention(q, k, v).reshape(bsz * L, SB_WIDTH)
    wb = w_out.astype(jnp.bfloat16)
    out = _proj_res_ln([([y_a], wb[:o]), ([y_b], wb[o:])], x2, ln_g, ln_b)
    return out.reshape(bsz, L, D)


NEG_BIG = -1e30
NSA_TQ = 256
NSA_TK = 256
N_SEL = 32
N_CMP_PAD = 128
SEG = CMP_STRIDE


def _gated_heads(o, gl):
    o = o * jax.nn.sigmoid(gl)
    return jnp.concatenate([o[r] for r in range(o.shape[0])], axis=1)


def _nsa_attn_kernel(q_ref, k_ref, v_ref, gl_ref, *rest, mode):
    if mode == "sel":
        sel_ref, o_ref = rest
    else:
        (o_ref,) = rest
    f32 = jnp.float32
    g = pl.program_id(1)
    i = pl.program_id(2)
    n_head, tq, hd = q_ref.shape[1:]
    tk = NSA_TK
    t0 = i * tq
    qs = [q_ref[0, r] for r in range(n_head)]
    tpos = t0 + lax.broadcasted_iota(jnp.int32, (tq, tk), 0)
    scale = hd ** -0.5
    if mode == "sel":
        sel = sel_ref[0, 0]
        lo = 0
    else:
        lo = jnp.maximum(t0 - WINDOW, 0) // tk
    hi = (t0 + tq + tk - 1) // tk

    def chunk(kc, carry):
        k0 = pl.multiple_of(kc * tk, tk)
        k = k_ref[0, 0, pl.ds(k0, tk), :]
        v = v_ref[0, 0, pl.ds(k0, tk), :]
        dist = tpos - (k0 + lax.broadcasted_iota(jnp.int32, (tq, tk), 1))
        valid = dist >= 0
        if mode == "win":
            valid = valid & (dist < WINDOW)
        else:
            blk = lax.broadcasted_iota(jnp.int32, (N_SEL, tk), 0)
            kblk = (k0 + lax.broadcasted_iota(jnp.int32, (N_SEL, tk), 1)) // SEL_BLOCK
            picked = jnp.dot(sel, (blk == kblk).astype(f32), preferred_element_type=f32)
            valid = valid & (picked > 0.5)
        distf = dist.astype(f32)
        out = []
        for r in range(n_head):
            m, l, acc = carry[r]
            slope = jnp.exp((g * n_head + r + 1).astype(f32) * (-8.0 / NSA_HEADS * math.log(2.0)))
            s = lax.dot_general(qs[r], k, (((1,), (1,)), ((), ())), preferred_element_type=f32)
            s = jnp.where(valid, s * scale - slope * distf, NEG_BIG)
            m_new = jnp.maximum(m, jnp.max(s, axis=1, keepdims=True))
            alpha = jnp.exp(m - m_new)
            p = jnp.where(valid, jnp.exp(s - m_new), 0.0)
            l = alpha * l + jnp.sum(p, axis=1, keepdims=True)
            acc = alpha * acc + jnp.dot(p.astype(v.dtype), v, preferred_element_type=f32)
            out.append((m_new, l, acc))
        return tuple(out)

    init = tuple((jnp.full((tq, 1), NEG_BIG, f32), jnp.zeros((tq, 1), f32), jnp.zeros((tq, hd), f32))
                 for _ in range(n_head))
    res = lax.fori_loop(lo, hi, chunk, init)
    o = jnp.stack([acc / l for _, l, acc in res], axis=0)
    o_ref[0] = _gated_heads(o, gl_ref[0])


def _nsa_attn(q, k, v, gl, sel, mode):
    b, h, L, hd = q.shape
    g = k.shape[1]
    r = h // g
    tq = NSA_TQ
    in_specs = [pl.BlockSpec((1, r, tq, hd), lambda bi, gi, i: (bi, gi, i, 0)),
                pl.BlockSpec((1, 1, L, hd), lambda bi, gi, i: (bi, gi, 0, 0)),
                pl.BlockSpec((1, 1, L, hd), lambda bi, gi, i: (bi, gi, 0, 0)),
                pl.BlockSpec((1, r, tq, 1), lambda bi, gi, i: (bi, gi, i, 0))]
    args = [q, k, v, gl]
    if mode == "sel":
        in_specs.append(pl.BlockSpec((1, 1, tq, N_SEL), lambda bi, gi, i: (bi, gi, i, 0)))
        args.append(sel)
    return pl.pallas_call(
        functools.partial(_nsa_attn_kernel, mode=mode),
        grid=(b, g, L // tq),
        in_specs=in_specs,
        out_specs=pl.BlockSpec((1, tq, r * hd), lambda bi, gi, i: (bi, i, gi)),
        out_shape=jax.ShapeDtypeStruct((b, L, h * hd), jnp.float32),
        compiler_params=pltpu.CompilerParams(dimension_semantics=("parallel", "parallel", "arbitrary"),
                                             vmem_limit_bytes=VMEM_LIMIT),
        name="nsa_" + mode,
    )(*args)


def _nsa_cmp_kernel(q_ref, kseg_ref, vseg_ref, gl_ref, wk_ref, wv_ref, bk_ref, bv_ref, w2k_ref, w2v_ref,
                    o_ref, sel_ref, kc_scr, vc_scr):
    f32 = jnp.float32
    bf16 = jnp.bfloat16
    g = pl.program_id(1)
    n_head, L, hd = q_ref.shape[1:]
    n_cmp = N_CMP_PAD

    def compress(seg_ref, w_ref, b_ref, w2_ref):
        seg = seg_ref[0, 0]
        first = jnp.dot(seg, w_ref[0], preferred_element_type=f32)
        second = jnp.dot(seg, w_ref[1], preferred_element_type=f32)
        hid = first + pltpu.roll(second, n_cmp - 1, 0) + b_ref[...]
        return jnp.dot(jax.nn.gelu(hid).astype(bf16), w2_ref[...], preferred_element_type=f32)

    kc_scr[...] = compress(kseg_ref, wk_ref, bk_ref, w2k_ref).astype(bf16)
    vc_scr[...] = compress(vseg_ref, wv_ref, bv_ref, w2v_ref).astype(bf16)
    scale = hd ** -0.5
    tq = NSA_TQ
    rows = n_head * tq
    head = lax.broadcasted_iota(jnp.int32, (n_head, tq, 1), 0).reshape(rows, 1)
    slope = jnp.exp((g * n_head + head + 1).astype(f32) * (-8.0 / NSA_HEADS * math.log(2.0)))
    ends = lax.broadcasted_iota(jnp.int32, (1, n_cmp), 1) * CMP_STRIDE + (CMP_BLOCK - 1)
    cs = lax.broadcasted_iota(jnp.int32, (N_SEL, n_cmp), 1) * CMP_STRIDE
    ss = lax.broadcasted_iota(jnp.int32, (N_SEL, n_cmp), 0) * SEL_BLOCK
    overlap = jnp.maximum(jnp.minimum(cs + CMP_BLOCK, ss + SEL_BLOCK) - jnp.maximum(cs, ss), 0)
    overlap = overlap.astype(f32) * (1.0 / CMP_BLOCK)
    jrow = lax.broadcasted_iota(jnp.int32, (N_SEL, tq), 0)

    def qblock(i, carry):
        t0 = pl.multiple_of(i * tq, tq)
        q = q_ref[0, :, pl.ds(t0, tq), :].reshape(rows, hd)
        tpos = t0 + lax.broadcasted_iota(jnp.int32, (n_head, tq, 1), 1).reshape(rows, 1)
        dist = tpos - ends
        s = lax.dot_general(q, kc_scr[...], (((1,), (1,)), ((), ())), preferred_element_type=f32) * scale
        s = s - slope * dist.astype(f32)
        valid = dist >= 0
        s = jnp.where(valid, s, NEG_BIG)
        m = jnp.max(s, axis=1, keepdims=True)
        e = jnp.where(valid, jnp.exp(s - m), 0.0)
        p = e / jnp.maximum(jnp.sum(e, axis=1, keepdims=True), 1e-30)
        o = jnp.dot(p.astype(bf16), vc_scr[...], preferred_element_type=f32)
        o_ref[0, pl.ds(t0, tq), :] = _gated_heads(o.reshape(n_head, tq, hd), gl_ref[0, :, pl.ds(t0, tq), :])
        psum = jnp.sum(p.reshape(n_head, tq, n_cmp), axis=0)
        imp = lax.dot_general(overlap, psum, (((1,), (1,)), ((), ())), preferred_element_type=f32,
                              precision=lax.Precision.HIGHEST)
        tcol = t0 + lax.broadcasted_iota(jnp.int32, (N_SEL, tq), 1)
        cur = tcol // SEL_BLOCK
        forced = (jrow == 0) | (jrow == cur) | (jrow == cur - 1)
        imp = jnp.where(jrow * SEL_BLOCK <= tcol, imp + jnp.where(forced, FORCE_BONUS, 0.0), -jnp.inf)
        rank = jnp.zeros((N_SEL, tq), jnp.int32)
        for j in range(N_SEL):
            other = imp[j:j + 1]
            ahead = (other > imp) | ((other == imp) & (j < jrow))
            rank = rank + ahead.astype(jnp.int32)
        sel_ref[0, 0, pl.ds(t0, tq), :] = (rank < SEL_TOPK).astype(f32).T
        return carry

    lax.fori_loop(0, L // tq, qblock, 0)


def _nsa_cmp(q, kseg, vseg, gl, wk, wv, bk, bv, w2k, w2v):
    b, h, L, hd = q.shape
    g = kseg.shape[1]
    r = h // g
    full = lambda a: pl.BlockSpec(a.shape, lambda bi, gi: (0,) * a.ndim)
    seg_spec = pl.BlockSpec((1, 1) + kseg.shape[2:], lambda bi, gi: (bi, gi, 0, 0))
    return pl.pallas_call(
        _nsa_cmp_kernel,
        grid=(b, g),
        in_specs=[pl.BlockSpec((1, r, L, hd), lambda bi, gi: (bi, gi, 0, 0)), seg_spec, seg_spec,
                  pl.BlockSpec((1, r, L, 1), lambda bi, gi: (bi, gi, 0, 0)),
                  full(wk), full(wv), full(bk), full(bv), full(w2k), full(w2v)],
        out_specs=[pl.BlockSpec((1, L, r * hd), lambda bi, gi: (bi, 0, gi)),
                   pl.BlockSpec((1, 1, L, N_SEL), lambda bi, gi: (bi, gi, 0, 0))],
        out_shape=[jax.ShapeDtypeStruct((b, L, h * hd), jnp.float32),
                   jax.ShapeDtypeStruct((b, g, L, N_SEL), jnp.float32)],
        scratch_shapes=[pltpu.VMEM((N_CMP_PAD, hd), jnp.bfloat16), pltpu.VMEM((N_CMP_PAD, hd), jnp.bfloat16)],
        compiler_params=pltpu.CompilerParams(dimension_semantics=("parallel", "arbitrary"),
                                             vmem_limit_bytes=VMEM_LIMIT),
        name="nsa_cmp",
    )(q, kseg, vseg, gl, wk, wv, bk, bv, w2k, w2v)


def nsa_mixer(x, w_in, pe_k, w1_k, w2_k, pe_v, w1_v, w2_v, w_out, ln_g, ln_b):
    bf16 = jnp.bfloat16
    bsz, L, D = x.shape
    G, hd = NSA_KV_HEADS, NSA_HEAD_DIM
    x2 = x.reshape(bsz * L, D)
    proj = _matmul(x2, w_in).reshape(bsz, L, -1)
    q = _heads_first(proj[..., :NSA_Q_WIDTH], NSA_HEADS)
    n_kv = 2 * NSA_BRANCHES
    kv = proj[..., NSA_Q_WIDTH:NSA_Q_WIDTH + n_kv * NSA_KV_WIDTH]
    kv = kv.reshape(bsz, L, n_kv, G, hd).transpose(2, 0, 3, 1, 4).astype(bf16)
    kc, vc, ks, vs, kw, vw = [kv[j] for j in range(n_kv)]
    gl = proj[..., NSA_Q_WIDTH + n_kv * NSA_KV_WIDTH:].reshape(bsz, L, NSA_BRANCHES, NSA_HEADS)
    gl = gl.transpose(2, 0, 3, 1)[..., None]

    def halves(w1):
        return w1.reshape(2, SEG * hd, -1).astype(bf16)

    def pos_term(pe, w1):
        return jnp.dot(pe.reshape(1, CMP_BLOCK * hd), w1, precision="highest")

    kseg = kc.reshape(bsz, G, L // SEG, SEG * hd)
    vseg = vc.reshape(bsz, G, L // SEG, SEG * hd)
    o_cmp, sel = _nsa_cmp(q, kseg, vseg, gl[0], halves(w1_k), halves(w1_v), pos_term(pe_k, w1_k),
                          pos_term(pe_v, w1_v), w2_k.astype(bf16), w2_v.astype(bf16))
    o_slc = _nsa_attn(q, ks, vs, gl[1], sel, "sel")
    o_win = _nsa_attn(q, kw, vw, gl[2], None, "win")
    branches = [o.reshape(bsz * L, NSA_Q_WIDTH) for o in (o_cmp, o_slc, o_win)]
    return _proj_res_ln([(branches, w_out.astype(bf16))], x2, ln_g, ln_b).reshape(bsz, L, D)


PEER_TC = LANE
PEER_SEL = PEER_HEADS * PEER_TOPK
ROW_TILES = D_MODEL // LANE
HI_HALF = 0xFFFF0000
ROUTE_OFF_MASK = 0x7FFFFFFF
PEER_ROW_GROUP = 64
PEER_TOK_BATCH = 8
ROUTE_WORDS = PEER_TC * PEER_SEL


def _top16(s):
    n = s.shape[0]
    iota = lax.broadcasted_iota(jnp.int32, s.shape, 0)
    vals, idxs = [], []
    for _ in range(PEER_TOPK):
        m = jnp.max(s, axis=0, keepdims=True)
        idx = jnp.min(jnp.where(s == m, iota, n), axis=0, keepdims=True)
        vals.append(m)
        idxs.append(idx)
        s = jnp.where(iota == idx, -jnp.inf, s)
    return jnp.concatenate(vals, axis=0), jnp.concatenate(idxs, axis=0)


def _peer_topk_kernel(x_ref, wq_ref, keys_ref, code_ref, g_ref):
    f32 = jnp.float32
    t = x_ref.shape[0]
    xb = x_ref[...].astype(jnp.bfloat16)
    j8 = lax.broadcasted_iota(jnp.int32, (8, t), 0)
    j16 = lax.broadcasted_iota(jnp.int32, (16, t), 0)
    neg = -jnp.inf
    experts, gates = [], []
    for h in range(PEER_HEADS):
        q = jnp.dot(xb, wq_ref[:, h * PEER_KEY_DIM:(h + 1) * PEER_KEY_DIM],
                    preferred_element_type=f32).astype(jnp.bfloat16)
        dn = (((1,), (1,)), ((), ()))
        s1 = lax.dot_general(keys_ref[2 * h], q[:, :PEER_HALF], dn, preferred_element_type=f32)
        s2 = lax.dot_general(keys_ref[2 * h + 1], q[:, PEER_HALF:], dn, preferred_element_type=f32)
        v1, i1 = _top16(s1)
        v2, i2 = _top16(s2)
        cand = [v1[0:1] + v2]
        flat = [j16]
        pay = [i1[0:1] * PEER_KEYS + i2]
        for i in range(1, 8):
            keep = j8 < PEER_TOPK // (i + 1)
            cand.append(jnp.where(keep, v1[i:i + 1] + v2[0:8], neg))
            flat.append(i * PEER_TOPK + j8)
            pay.append(i1[i:i + 1] * PEER_KEYS + i2[0:8])
        cand.append(v1[8:16] + v2[0:1])
        flat.append((8 + j8) * PEER_TOPK)
        pay.append(i1[8:16] * PEER_KEYS + i2[0:1])
        cand = jnp.concatenate(cand, axis=0)
        flat = jnp.concatenate(flat, axis=0)
        pay = jnp.concatenate(pay, axis=0)
        tops = []
        for _ in range(PEER_TOPK):
            m = jnp.max(cand, axis=0, keepdims=True)
            fl = jnp.min(jnp.where(cand == m, flat, PEER_TOPK * PEER_TOPK), axis=0, keepdims=True)
            sel = flat == fl
            experts.append(jnp.max(jnp.where(sel, pay, -1), axis=0, keepdims=True))
            tops.append(m)
            cand = jnp.where(sel, neg, cand)
        top = jnp.concatenate(tops, axis=0)
        e = jnp.exp(top - top[0:1])
        gates.append(e / jnp.sum(e, axis=0, keepdims=True))
    expert = jnp.concatenate(experts, axis=0)
    code = ((expert >> 1) * ROW_TILES) | (((expert & 1) ^ 1) << 31)
    code_ref[0] = code.T
    g_ref[0] = jnp.concatenate(gates, axis=0)


def _peer_topk(x2, wq_b, keys_b):
    n_tok, d = x2.shape
    n_chunk = n_tok // PEER_TC
    return pl.pallas_call(
        _peer_topk_kernel,
        grid=(n_chunk,),
        in_specs=[pl.BlockSpec((PEER_TC, d), lambda i: (i, 0)),
                  pl.BlockSpec(wq_b.shape, lambda i: (0, 0)),
                  pl.BlockSpec(keys_b.shape, lambda i: (0, 0, 0))],
        out_specs=[pl.BlockSpec((1, PEER_TC, PEER_SEL), lambda i: (i, 0, 0)),
                   pl.BlockSpec((1, PEER_SEL, PEER_TC), lambda i: (i, 0, 0))],
        out_shape=[jax.ShapeDtypeStruct((n_chunk, PEER_TC, PEER_SEL), jnp.int32),
                   jax.ShapeDtypeStruct((n_chunk, PEER_SEL, PEER_TC), jnp.float32)],
        compiler_params=pltpu.CompilerParams(dimension_semantics=("arbitrary",), vmem_limit_bytes=VMEM_LIMIT),
        name="peer_topk",
    )(x2, wq_b, keys_b)


def _pack_table(tab):
    n, d = tab.shape
    b = lax.bitcast_convert_type(tab.astype(jnp.bfloat16), jnp.uint16).astype(jnp.uint32)
    b = b.reshape(n // 2, 2, d)
    return (b[:, 0] | (b[:, 1] << 16)).reshape(n // 2 * (d // LANE), LANE)


def _table_row(tab_ref, code):
    off = pl.multiple_of(code & ROUTE_OFF_MASK, ROW_TILES)
    t = tab_ref[pl.ds(off, ROW_TILES), :]
    sh = jnp.where(jnp.full(t.shape, code, jnp.int32) < 0, jnp.uint32(16), jnp.uint32(0))
    return pltpu.bitcast((t << sh) & jnp.uint32(HI_HALF), jnp.float32)


def _rowsum8(ps):
    sub = lax.broadcasted_iota(jnp.int32, (8, LANE), 0)
    for d in (1, 2, 4):
        keep = (sub & (2 * d - 1)) < d
        ps = [jnp.where(keep, a, b) + pltpu.roll(jnp.where(keep, b, a), d, 0)
              for a, b in zip(ps[0::2], ps[1::2])]
    return ps[0]


def _peer_prefetch(i, n, hbm_refs, smem_refs, sem, tab_hbm, tab_vmem, sem_tab):
    slot = i % 2

    def copies(chunk, s):
        return [pltpu.make_async_copy(h.at[pl.ds(chunk * ROUTE_WORDS, ROUTE_WORDS)],
                                      m.at[pl.ds(s * ROUTE_WORDS, ROUTE_WORDS)], sem.at[k, s])
                for k, (h, m) in enumerate(zip(hbm_refs, smem_refs))]

    @pl.when(i == 0)
    def _():
        tab = pltpu.make_async_copy(tab_hbm, tab_vmem, sem_tab)
        tab.start()
        for c in copies(0, 0):
            c.start()
        tab.wait()

    @pl.when(i + 1 < n)
    def _():
        for c in copies(i + 1, 1 - slot):
            c.start()

    for c in copies(i, slot):
        c.wait()
    return slot


def _peer_u_kernel(code_hbm, tab_hbm, x_ref, g_ref, w_ref, tab_vmem, code_s, r_scr, sem, sem_tab):
    i = pl.program_id(0)
    slot = _peer_prefetch(i, pl.num_programs(0), (code_hbm,), (code_s,), sem, tab_hbm, tab_vmem, sem_tab)
    t = x_ref.shape[0]
    lane = lax.broadcasted_iota(jnp.int32, (PEER_SEL, t), 1)

    def token_batch(cb, hacc):
        for j in range(PEER_TOK_BATCH):
            c = cb * PEER_TOK_BATCH + j
            xt = x_ref[c]

            def group(gi, carry, c=c, xt=xt, j=j):
                base = pl.multiple_of(gi * PEER_ROW_GROUP, PEER_ROW_GROUP)
                word = (slot * t + c) * PEER_SEL + base
                sums = [_rowsum8([_table_row(tab_vmem, code_s[word + b * 8 + r]) * xt for r in range(8)])
                        for b in range(PEER_ROW_GROUP // 8)]
                r_scr[pl.ds(j * PEER_SEL + base, PEER_ROW_GROUP), :] = jnp.concatenate(sums, axis=0)
                return carry

            lax.fori_loop(0, PEER_SEL // PEER_ROW_GROUP, group, 0)
        col = jnp.sum(r_scr[...], axis=1, keepdims=True)
        for j in range(PEER_TOK_BATCH):
            hacc = jnp.where(lane == cb * PEER_TOK_BATCH + j, col[j * PEER_SEL:(j + 1) * PEER_SEL], hacc)
        return hacc

    hacc = lax.fori_loop(0, t // PEER_TOK_BATCH, token_batch, jnp.zeros((PEER_SEL, t), jnp.float32))
    w_ref[0] = (g_ref[0] * jax.nn.gelu(hacc)).T


def _peer_v_kernel(code_hbm, w_hbm, tab_hbm, o_ref, tab_vmem, code_s, w_s, sem, sem_tab):
    i = pl.program_id(0)
    slot = _peer_prefetch(i, pl.num_programs(0), (code_hbm, w_hbm), (code_s, w_s), sem,
                          tab_hbm, tab_vmem, sem_tab)
    n_acc = 4
    zero = jnp.zeros((ROW_TILES, LANE), jnp.float32)

    def token(c, carry):
        def group(gi, accs):
            word = (slot * PEER_TC + c) * PEER_SEL + gi * PEER_ROW_GROUP
            accs = list(accs)
            for r in range(PEER_ROW_GROUP):
                term = _table_row(tab_vmem, code_s[word + r]) * w_s[word + r]
                accs[r % n_acc] = accs[r % n_acc] + term
            return tuple(accs)

        accs = lax.fori_loop(0, PEER_SEL // PEER_ROW_GROUP, group, (zero,) * n_acc)
        o_ref[c] = (accs[0] + accs[1]) + (accs[2] + accs[3])
        return carry

    lax.fori_loop(0, o_ref.shape[0], token, 0)


def _peer_gather_params():
    return pltpu.CompilerParams(dimension_semantics=("arbitrary",), vmem_limit_bytes=56 * 1024 * 1024)


def _peer_u(code, tab_u, x3, g):
    n_chunk = code.shape[0] // ROUTE_WORDS
    any_spec = pl.BlockSpec(memory_space=pl.ANY)
    return pl.pallas_call(
        _peer_u_kernel,
        grid=(n_chunk,),
        in_specs=[any_spec, any_spec,
                  pl.BlockSpec((PEER_TC, ROW_TILES, LANE), lambda i: (i, 0, 0)),
                  pl.BlockSpec((1, PEER_SEL, PEER_TC), lambda i: (i, 0, 0))],
        out_specs=pl.BlockSpec((1, PEER_TC, PEER_SEL), lambda i: (i, 0, 0)),
        out_shape=jax.ShapeDtypeStruct((n_chunk, PEER_TC, PEER_SEL), jnp.float32),
        scratch_shapes=[pltpu.VMEM(tab_u.shape, jnp.uint32),
                        pltpu.SMEM((2 * ROUTE_WORDS,), jnp.int32),
                        pltpu.VMEM((PEER_TOK_BATCH * PEER_SEL, LANE), jnp.float32),
                        pltpu.SemaphoreType.DMA((1, 2)), pltpu.SemaphoreType.DMA(())],
        compiler_params=_peer_gather_params(),
        name="peer_u",
    )(code, tab_u, x3, g)


def _peer_v(code, w, tab_v):
    n_chunk = code.shape[0] // ROUTE_WORDS
    any_spec = pl.BlockSpec(memory_space=pl.ANY)
    return pl.pallas_call(
        _peer_v_kernel,
        grid=(n_chunk,),
        in_specs=[any_spec, any_spec, any_spec],
        out_specs=pl.BlockSpec((PEER_TC, ROW_TILES, LANE), lambda i: (i, 0, 0)),
        out_shape=jax.ShapeDtypeStruct((n_chunk * PEER_TC, ROW_TILES, LANE), jnp.float32),
        scratch_shapes=[pltpu.VMEM(tab_v.shape, jnp.uint32),
                        pltpu.SMEM((2 * ROUTE_WORDS,), jnp.int32),
                        pltpu.SMEM((2 * ROUTE_WORDS,), jnp.float32),
                        pltpu.SemaphoreType.DMA((2, 2)), pltpu.SemaphoreType.DMA(())],
        compiler_params=_peer_gather_params(),
        name="peer_v",
    )(code, w, tab_v)


def peer_ffn(x, w_q, sub_keys, u_tab, v_tab, ln_g, ln_b):
    bsz, L, D = x.shape
    x2 = x.reshape(bsz * L, D)
    keys_b = sub_keys.reshape(PEER_HEADS * 2, PEER_KEYS, PEER_HALF).astype(jnp.bfloat16)
    code, g = _peer_topk(x2, w_q.astype(jnp.bfloat16), keys_b)
    code = code.reshape(-1)
    w = _peer_u(code, _pack_table(u_tab), x2.reshape(bsz * L, ROW_TILES, LANE), g)
    out = _peer_v(code, w.reshape(-1), _pack_table(v_tab))
    return _res_ln(x2, out.reshape(bsz * L, D), ln_g, ln_b).reshape(bsz, L, D)


def kernel(x, even_w_in, even_lam_re, even_lam_im, even_log_dt, even_b_re, even_b_im, even_c_re, even_c_im, even_d, even_w_glu, even_b_glu, even_w_out, odd_w_in, odd_pe_k, odd_w1_k, odd_w2_k, odd_pe_v, odd_w1_v, odd_w2_v, odd_w_out, peer_w_q, peer_sub_keys, peer_u, peer_v, ln1_g, ln1_b, ln2_g, ln2_b):
    h = x
    for i in range(DEPTH):
        j = i // 2
        if i % 2 == 0:
            h = even_mixer(h, even_w_in[j], even_lam_re[j], even_lam_im[j], even_log_dt[j],
                           even_b_re[j], even_b_im[j], even_c_re[j], even_c_im[j], even_d[j],
                           even_w_glu[j], even_b_glu[j], even_w_out[j], ln1_g[i], ln1_b[i])
        else:
            h = nsa_mixer(h, odd_w_in[j], odd_pe_k[j], odd_w1_k[j], odd_w2_k[j],
                          odd_pe_v[j], odd_w1_v[j], odd_w2_v[j], odd_w_out[j], ln1_g[i], ln1_b[i])
        h = peer_ffn(h, peer_w_q[i], peer_sub_keys[i], peer_u[i], peer_v[i], ln2_g[i], ln2_b[i])
    return h
```

```python
import functools
import math

import numpy as np
import jax
import jax.numpy as jnp
from jax import lax
from jax.experimental import pallas as pl
from jax.experimental.pallas import tpu as pltpu

D_MODEL = 1024
DEPTH = 2
S5_GROUP = 16
S5_CHANNELS = D_MODEL // 4
S5_GROUPS = S5_CHANNELS // S5_GROUP
S5_STATE = 64
SB_HEAD_DIM = 64
SB_HEADS = (D_MODEL - S5_CHANNELS) // SB_HEAD_DIM
SB_WIDTH = SB_HEADS * SB_HEAD_DIM
NSA_HEADS = 16
NSA_KV_HEADS = 4
NSA_GROUP = NSA_HEADS // NSA_KV_HEADS
NSA_HEAD_DIM = 64
NSA_Q_WIDTH = NSA_HEADS * NSA_HEAD_DIM
NSA_KV_WIDTH = NSA_KV_HEADS * NSA_HEAD_DIM
NSA_BRANCHES = 3
CMP_BLOCK = 32
CMP_STRIDE = 16
SEL_BLOCK = 64
SEL_TOPK = 16
WINDOW = 512
FORCE_BONUS = 1000.0
PEER_HEADS = 8
PEER_KEYS = 128
PEER_KEY_DIM = 256
PEER_HALF = PEER_KEY_DIM // 2
PEER_TOPK = 16
DN_ALPHA = (2 * DEPTH) ** 0.25
LN_EPS = 1e-5

LANE = 128
VMEM_LIMIT = 48 * 1024 * 1024


def _mm_kernel(x_ref, w_ref, o_ref):
    o_ref[...] = jnp.dot(x_ref[...].astype(jnp.bfloat16), w_ref[...].astype(jnp.bfloat16),
                         preferred_element_type=jnp.float32)


def _matmul(x, w, tm=512, tn=512):
    m, k = x.shape
    n = w.shape[1]
    n_pad = -(-n // LANE) * LANE
    if n_pad != n:
        w = jnp.pad(w, ((0, 0), (0, n_pad - n)))
    tn = max(t for t in range(LANE, tn + 1, LANE) if n_pad % t == 0)
    assert m % tm == 0
    out = pl.pallas_call(
        _mm_kernel,
        grid=(m // tm, n_pad // tn),
        in_specs=[pl.BlockSpec((tm, k), lambda i, j: (i, 0)),
                  pl.BlockSpec((k, tn), lambda i, j: (0, j))],
        out_specs=pl.BlockSpec((tm, tn), lambda i, j: (i, j)),
        out_shape=jax.ShapeDtypeStruct((m, n_pad), jnp.float32),
        name="matmul",
    )(x, w)
    return out[:, :n] if n_pad != n else out


def _ln_rows(y, g, b):
    mu = jnp.mean(y, axis=-1, keepdims=True)
    d = y - mu
    var = jnp.mean(d * d, axis=-1, keepdims=True)
    return d * lax.rsqrt(var + LN_EPS) * g + b


def _res_ln_kernel(h_ref, m_ref, g_ref, b_ref, o_ref):
    o_ref[...] = _ln_rows(DN_ALPHA * h_ref[...] + m_ref[...], g_ref[...], b_ref[...])


def _res_ln(h2, mix2, g, b, tm=512):
    m, d = h2.shape
    row = pl.BlockSpec((tm, d), lambda i: (i, 0))
    vec = pl.BlockSpec((1, d), lambda i: (0, 0))
    return pl.pallas_call(
        _res_ln_kernel,
        grid=(m // tm,),
        in_specs=[row, row, vec, vec],
        out_specs=row,
        out_shape=jax.ShapeDtypeStruct((m, d), jnp.float32),
        compiler_params=pltpu.CompilerParams(dimension_semantics=("parallel",), vmem_limit_bytes=VMEM_LIMIT),
        name="res_ln",
    )(h2, mix2, g.reshape(1, d), b.reshape(1, d))


def _proj_res_ln_kernel(*refs, n_sum):
    n_x = sum(n_sum)
    xs, ws = refs[:n_x], refs[n_x:n_x + len(n_sum)]
    h_ref, g_ref, b_ref, o_ref = refs[n_x + len(n_sum):]
    acc = DN_ALPHA * h_ref[...]
    k = 0
    for cnt, w_ref in zip(n_sum, ws):
        x = xs[k][...]
        for j in range(1, cnt):
            x = x + xs[k + j][...]
        k += cnt
        acc = acc + jnp.dot(x.astype(jnp.bfloat16), w_ref[...], preferred_element_type=jnp.float32)
    o_ref[...] = _ln_rows(acc, g_ref[...], b_ref[...])


def _proj_res_ln(groups, h2, g, b, tm=512):
    m, d = h2.shape
    xs, ws, n_sum, in_specs = [], [], [], []
    for arrs, w in groups:
        n_sum.append(len(arrs))
        for a in arrs:
            xs.append(a)
            in_specs.append(pl.BlockSpec((tm, a.shape[1]), lambda i: (i, 0)))
    for _, w in groups:
        ws.append(w)
        in_specs.append(pl.BlockSpec(w.shape, lambda i: (0, 0)))
    row = pl.BlockSpec((tm, d), lambda i: (i, 0))
    vec = pl.BlockSpec((1, d), lambda i: (0, 0))
    return pl.pallas_call(
        functools.partial(_proj_res_ln_kernel, n_sum=tuple(n_sum)),
        grid=(m // tm,),
        in_specs=in_specs + [row, vec, vec],
        out_specs=row,
        out_shape=jax.ShapeDtypeStruct((m, d), jnp.float32),
        compiler_params=pltpu.CompilerParams(dimension_semantics=("parallel",), vmem_limit_bytes=VMEM_LIMIT),
        name="proj_res_ln",
    )(*xs, *ws, h2, g.reshape(1, d), b.reshape(1, d))


S5_T = 64
S5_FLAT = S5_GROUP * S5_T
S5_REAL = 2 * S5_STATE


def _s5_operators(lam_re, lam_im, log_dt, b_re, b_im, c_re, c_im):
    f32 = jnp.float32
    T = S5_T
    lr = lam_re.astype(f32)
    li = lam_im.astype(f32)
    dt = jnp.exp(log_dt.astype(f32))[:, None]
    mag = jnp.exp(lr * dt)
    ab_re = mag * jnp.cos(li * dt)
    ab_im = mag * jnp.sin(li * dt)
    nr = ab_re - 1.0
    den = lr * lr + li * li
    f_re = (nr * lr + ab_im * li) / den
    f_im = (ab_im * lr - nr * li) / den
    bb_re = f_re[..., None] * b_re.astype(f32) - f_im[..., None] * b_im.astype(f32)
    bb_im = f_re[..., None] * b_im.astype(f32) + f_im[..., None] * b_re.astype(f32)
    cr = c_re.astype(f32)
    ci = c_im.astype(f32)
    n = jnp.arange(T + 1, dtype=f32)[None, :, None]
    pw_mag = jnp.exp(n * (lr * dt)[:, None, :])
    pw_re = pw_mag * jnp.cos(n * (li * dt)[:, None, :])
    pw_im = pw_mag * jnp.sin(n * (li * dt)[:, None, :])
    hp = "highest"
    cb_re = cr[:, :, :, None] * bb_re[:, None] - ci[:, :, :, None] * bb_im[:, None]
    cb_im = cr[:, :, :, None] * bb_im[:, None] + ci[:, :, :, None] * bb_re[:, None]
    K = (jnp.einsum('gtp,gipj->gtij', pw_re[:, :T], cb_re, precision=hp)
         - jnp.einsum('gtp,gipj->gtij', pw_im[:, :T], cb_im, precision=hp))
    s_idx = jnp.arange(T)[:, None]
    t_idx = jnp.arange(T)[None, :]
    lag = t_idx - s_idx
    Kst = jnp.where((lag >= 0)[None, :, :, None, None], K[:, jnp.clip(lag, 0, T - 1)], 0.0)
    M = Kst.transpose(0, 4, 1, 3, 2).reshape(-1, S5_FLAT, S5_FLAT)
    rev_re = pw_re[:, :T][:, ::-1]
    rev_im = pw_im[:, :T][:, ::-1]
    bt_re = bb_re.transpose(0, 2, 1)[:, :, None]
    bt_im = bb_im.transpose(0, 2, 1)[:, :, None]
    F_re = rev_re[:, None] * bt_re - rev_im[:, None] * bt_im
    F_im = rev_re[:, None] * bt_im + rev_im[:, None] * bt_re
    F = jnp.concatenate([F_re, F_im], axis=-1).reshape(-1, S5_FLAT, S5_REAL)
    up_re = pw_re[:, 1:]
    up_im = pw_im[:, 1:]
    E_re = cr[:, :, None] * up_re[:, None] - ci[:, :, None] * up_im[:, None]
    E_im = -(cr[:, :, None] * up_im[:, None] + ci[:, :, None] * up_re[:, None])
    E = jnp.concatenate([E_re, E_im], axis=-1).reshape(-1, S5_FLAT, S5_REAL).transpose(0, 2, 1)
    AT = jnp.stack([jnp.concatenate([pw_re[:, T], pw_re[:, T]], axis=-1),
                    jnp.concatenate([-pw_im[:, T], pw_im[:, T]], axis=-1)], axis=1)
    bf16 = jnp.bfloat16
    return M.astype(bf16), F.astype(bf16), E.astype(bf16), AT


def _s5_kernel(u_ref, m_ref, f_ref, e_ref, at_ref, y_ref, sin_scr, *, n_chunk, bsz):
    f32 = jnp.float32
    u = u_ref[0]
    sloc = jnp.dot(u, f_ref[0], preferred_element_type=f32)
    a_main = at_ref[0, 0:1]
    a_cross = at_ref[0, 1:2]
    carry = jnp.zeros((bsz, S5_REAL), f32)
    for c in range(n_chunk):
        sin_scr[c * bsz:(c + 1) * bsz, :] = carry
        carry = carry * a_main + pltpu.roll(carry, S5_STATE, 1) * a_cross + sloc[c * bsz:(c + 1) * bsz]
    y = jnp.dot(u, m_ref[0], preferred_element_type=f32)
    y_ref[0] = y + jnp.dot(sin_scr[...].astype(jnp.bfloat16), e_ref[0], preferred_element_type=f32)


def _s5_scan(u, ops):
    M, F, E, AT = ops
    bsz, L, _ = u.shape
    n_chunk = L // S5_T
    rows = n_chunk * bsz
    uf = u.reshape(bsz, n_chunk, S5_T, S5_GROUPS, S5_GROUP).transpose(3, 1, 0, 4, 2)
    uf = uf.reshape(S5_GROUPS, rows, S5_FLAT).astype(jnp.bfloat16)
    grp = lambda a: pl.BlockSpec((1,) + a.shape[1:], lambda g: (g,) + (0,) * (a.ndim - 1))
    yf = pl.pallas_call(
        functools.partial(_s5_kernel, n_chunk=n_chunk, bsz=bsz),
        grid=(S5_GROUPS,),
        in_specs=[grp(uf), grp(M), grp(F), grp(E), grp(AT)],
        out_specs=pl.BlockSpec((1, rows, S5_FLAT), lambda g: (g, 0, 0)),
        out_shape=jax.ShapeDtypeStruct((S5_GROUPS, rows, S5_FLAT), jnp.float32),
        scratch_shapes=[pltpu.VMEM((rows, S5_REAL), jnp.float32)],
        compiler_params=pltpu.CompilerParams(dimension_semantics=("parallel",), vmem_limit_bytes=VMEM_LIMIT),
        name="s5_scan",
    )(uf, M, F, E, AT)
    yf = yf.reshape(S5_GROUPS, n_chunk, bsz, S5_GROUP, S5_T).transpose(2, 1, 4, 0, 3)
    return yf.reshape(bsz, L, S5_CHANNELS)


def _s5_post_kernel(y_ref, u_ref, d_ref, wg_ref, bg_ref, o_ref):
    z = jax.nn.gelu(y_ref[...] + d_ref[...] * u_ref[...])
    gate = jnp.dot(z.astype(jnp.bfloat16), wg_ref[...], preferred_element_type=jnp.float32) + bg_ref[...]
    o_ref[...] = z * jax.nn.sigmoid(gate)


def _s5_post(y2, proj2, d_skip, w_glu, b_glu, tm=512):
    m = y2.shape[0]
    c = S5_CHANNELS
    row = pl.BlockSpec((tm, c), lambda i: (i, 0))
    vec = pl.BlockSpec((1, c), lambda i: (0, 0))
    return pl.pallas_call(
        _s5_post_kernel,
        grid=(m // tm,),
        in_specs=[row, row, vec, pl.BlockSpec((c, c), lambda i: (0, 0)), vec],
        out_specs=row,
        out_shape=jax.ShapeDtypeStruct((m, c), jnp.float32),
        name="s5_post",
    )(y2, proj2, d_skip.reshape(1, c), w_glu.astype(jnp.bfloat16), b_glu.reshape(1, c))


SB_TQ = 256
SB_PAIR = LANE // SB_HEAD_DIM


def _sb_kernel(q_ref, k_ref, v_ref, o_ref):
    f32 = jnp.float32
    bf16 = jnp.bfloat16
    i = pl.program_id(2)
    n_head, tq, hd = q_ref.shape[1:]
    tk = tq
    scale = hd ** -0.5
    t0 = i * tq
    tpos = t0 + lax.broadcasted_iota(jnp.int32, (tq, tk), 0)
    col = lax.broadcasted_iota(jnp.int32, (tk, tk), 1)
    row = lax.broadcasted_iota(jnp.int32, (tk, tk), 0)
    later = (row > col).astype(bf16)
    qs = [q_ref[0, hh] for hh in range(n_head)]

    def chunk(step, carry):
        k0 = pl.multiple_of((i - step) * tk, tk)
        mask = (k0 + lax.broadcasted_iota(jnp.int32, (tq, tk), 1)) < tpos
        out = []
        for hh in range(n_head):
            run, acc = carry[hh]
            k = k_ref[0, hh, pl.ds(k0, tk), :]
            v = v_ref[0, hh, pl.ds(k0, tk), :]
            z = lax.dot_general(qs[hh], k, (((1,), (1,)), ((), ())), preferred_element_type=f32) * scale
            log_sig = jnp.minimum(z, 0.0) - jnp.log(1.0 + jnp.exp(-jnp.abs(z)))
            log1m = jnp.where(mask, log_sig - z, 0.0)
            hi = log1m.astype(bf16)
            lo = (log1m - hi.astype(f32)).astype(bf16)
            tail = jnp.dot(hi, later, preferred_element_type=f32) + jnp.dot(lo, later, preferred_element_type=f32)
            w = jnp.where(mask, jnp.exp(log_sig + tail + run), 0.0)
            acc = acc + jnp.dot(w.astype(bf16), v, preferred_element_type=f32)
            run = run + jnp.sum(log1m, axis=1, keepdims=True)
            out.append((run, acc))
        return tuple(out)

    init = tuple((jnp.zeros((tq, 1), f32), jnp.zeros((tq, hd), f32)) for _ in range(n_head))
    res = lax.fori_loop(0, i + 1, chunk, init)
    o_ref[0] = jnp.concatenate([acc for _, acc in res], axis=1)


def _sb_attention(q, k, v):
    b, h, L, hd = q.shape
    tq = SB_TQ
    kv_spec = pl.BlockSpec((1, SB_PAIR, L, hd), lambda bi, hi, i: (bi, hi, 0, 0))
    return pl.pallas_call(
        _sb_kernel,
        grid=(b, h // SB_PAIR, L // tq),
        in_specs=[pl.BlockSpec((1, SB_PAIR, tq, hd), lambda bi, hi, i: (bi, hi, i, 0)), kv_spec, kv_spec],
        out_specs=pl.BlockSpec((1, tq, SB_PAIR * hd), lambda bi, hi, i: (bi, i, hi)),
        out_shape=jax.ShapeDtypeStruct((b, L, h * hd), jnp.float32),
        compiler_params=pltpu.CompilerParams(dimension_semantics=("parallel", "parallel", "arbitrary"),
                                             vmem_limit_bytes=VMEM_LIMIT),
        name="sb_attn",
    )(q, k, v)


def _heads_first(a, n_head):
    b, L, w = a.shape
    return a.reshape(b, L, n_head, w // n_head).transpose(0, 2, 1, 3).astype(jnp.bfloat16)


def even_mixer(x, w_in, lam_re, lam_im, log_dt, b_re, b_im, c_re, c_im, d_skip, w_glu, b_glu, w_out,
               ln_g, ln_b):
    bsz, L, D = x.shape
    x2 = x.reshape(bsz * L, D)
    proj = _matmul(x2, w_in)
    p3 = proj.reshape(bsz, L, -1)
    o = S5_CHANNELS
    y = _s5_scan(p3[..., :o], _s5_operators(lam_re, lam_im, log_dt, b_re, b_im, c_re, c_im))
    y_a = _s5_post(y.reshape(bsz * L, o), proj, d_skip, w_glu, b_glu)
    q = _heads_first(p3[..., o:o + SB_WIDTH], SB_HEADS)
    k = _heads_first(p3[..., o + SB_WIDTH:o + 2 * SB_WIDTH], SB_HEADS)
    v = _heads_first(p3[..., o + 2 * SB_WIDTH:], SB_HEADS)
    y_b = _sb_attention(q, k, v).reshape(bsz * L, SB_WIDTH)
    wb = w_out.astype(jnp.bfloat16)
    out = _proj_res_ln([([y_a], wb[:o]), ([y_b], wb[o:])], x2, ln_g, ln_b)
    return out.reshape(bsz, L, D)


NEG_BIG = -1e30
NSA_TQ = 256
NSA_TK = 256
N_SEL = 32
N_CMP_PAD = 128
SEG = CMP_STRIDE


def _gated_heads(o, gl):
    o = o * jax.nn.sigmoid(gl)
    return jnp.concatenate([o[r] for r in range(o.shape[0])], axis=1)


def _nsa_attn_kernel(q_ref, k_ref, v_ref, gl_ref, *rest, mode):
    if mode == "sel":
        sel_ref, o_ref = rest
    else:
        (o_ref,) = rest
    f32 = jnp.float32
    g = pl.program_id(1)
    i = pl.program_id(2)
    n_head, tq, hd = q_ref.shape[1:]
    tk = NSA_TK
    t0 = i * tq
    qs = [q_ref[0, r] for r in range(n_head)]
    tpos = t0 + lax.broadcasted_iota(jnp.int32, (tq, tk), 0)
    scale = hd ** -0.5
    if mode == "sel":
        sel = sel_ref[0, 0]
        lo = 0
    else:
        lo = jnp.maximum(t0 - WINDOW, 0) // tk
    hi = (t0 + tq + tk - 1) // tk

    def chunk(kc, carry):
        k0 = pl.multiple_of(kc * tk, tk)
        k = k_ref[0, 0, pl.ds(k0, tk), :]
        v = v_ref[0, 0, pl.ds(k0, tk), :]
        dist = tpos - (k0 + lax.broadcasted_iota(jnp.int32, (tq, tk), 1))
        valid = dist >= 0
        if mode == "win":
            valid = valid & (dist < WINDOW)
        else:
            blk = lax.broadcasted_iota(jnp.int32, (N_SEL, tk), 0)
            kblk = (k0 + lax.broadcasted_iota(jnp.int32, (N_SEL, tk), 1)) // SEL_BLOCK
            picked = jnp.dot(sel, (blk == kblk).astype(f32), preferred_element_type=f32)
            valid = valid & (picked > 0.5)
        distf = dist.astype(f32)
        out = []
        for r in range(n_head):
            m, l, acc = carry[r]
            slope = jnp.exp((g * n_head + r + 1).astype(f32) * (-8.0 / NSA_HEADS * math.log(2.0)))
            s = lax.dot_general(qs[r], k, (((1,), (1,)), ((), ())), preferred_element_type=f32)
            s = jnp.where(valid, s * scale - slope * distf, NEG_BIG)
            m_new = jnp.maximum(m, jnp.max(s, axis=1, keepdims=True))
            alpha = jnp.exp(m - m_new)
            p = jnp.where(valid, jnp.exp(s - m_new), 0.0)
            l = alpha * l + jnp.sum(p, axis=1, keepdims=True)
            acc = alpha * acc + jnp.dot(p.astype(v.dtype), v, preferred_element_type=f32)
            out.append((m_new, l, acc))
        return tuple(out)

    init = tuple((jnp.full((tq, 1), NEG_BIG, f32), jnp.zeros((tq, 1), f32), jnp.zeros((tq, hd), f32))
                 for _ in range(n_head))
    res = lax.fori_loop(lo, hi, chunk, init)
    o = jnp.stack([acc / l for _, l, acc in res], axis=0)
    o_ref[0] = _gated_heads(o, gl_ref[0])


def _nsa_attn(q, k, v, gl, sel, mode):
    b, h, L, hd = q.shape
    g = k.shape[1]
    r = h // g
    tq = NSA_TQ
    in_specs = [pl.BlockSpec((1, r, tq, hd), lambda bi, gi, i: (bi, gi, i, 0)),
                pl.BlockSpec((1, 1, L, hd), lambda bi, gi, i: (bi, gi, 0, 0)),
                pl.BlockSpec((1, 1, L, hd), lambda bi, gi, i: (bi, gi, 0, 0)),
                pl.BlockSpec((1, r, tq, 1), lambda bi, gi, i: (bi, gi, i, 0))]
    args = [q, k, v, gl]
    if mode == "sel":
        in_specs.append(pl.BlockSpec((1, 1, tq, N_SEL), lambda bi, gi, i: (bi, gi, i, 0)))
        args.append(sel)
    return pl.pallas_call(
        functools.partial(_nsa_attn_kernel, mode=mode),
        grid=(b, g, L // tq),
        in_specs=in_specs,
        out_specs=pl.BlockSpec((1, tq, r * hd), lambda bi, gi, i: (bi, i, gi)),
        out_shape=jax.ShapeDtypeStruct((b, L, h * hd), jnp.float32),
        compiler_params=pltpu.CompilerParams(dimension_semantics=("parallel", "parallel", "arbitrary"),
                                             vmem_limit_bytes=VMEM_LIMIT),
        name="nsa_" + mode,
    )(*args)


def _nsa_cmp_kernel(q_ref, kseg_ref, vseg_ref, gl_ref, wk_ref, wv_ref, bk_ref, bv_ref, w2k_ref, w2v_ref,
                    o_ref, sel_ref, kc_scr, vc_scr):
    f32 = jnp.float32
    bf16 = jnp.bfloat16
    g = pl.program_id(1)
    n_head, L, hd = q_ref.shape[1:]
    n_cmp = N_CMP_PAD

    def compress(seg_ref, w_ref, b_ref, w2_ref):
        seg = seg_ref[0, 0]
        first = jnp.dot(seg, w_ref[0], preferred_element_type=f32)
        second = jnp.dot(seg, w_ref[1], preferred_element_type=f32)
        hid = first + pltpu.roll(second, n_cmp - 1, 0) + b_ref[...]
        return jnp.dot(jax.nn.gelu(hid).astype(bf16), w2_ref[...], preferred_element_type=f32)

    kc_scr[...] = compress(kseg_ref, wk_ref, bk_ref, w2k_ref).astype(bf16)
    vc_scr[...] = compress(vseg_ref, wv_ref, bv_ref, w2v_ref).astype(bf16)
    scale = hd ** -0.5
    tq = NSA_TQ
    rows = n_head * tq
    head = lax.broadcasted_iota(jnp.int32, (n_head, tq, 1), 0).reshape(rows, 1)
    slope = jnp.exp((g * n_head + head + 1).astype(f32) * (-8.0 / NSA_HEADS * math.log(2.0)))
    ends = lax.broadcasted_iota(jnp.int32, (1, n_cmp), 1) * CMP_STRIDE + (CMP_BLOCK - 1)
    cs = lax.broadcasted_iota(jnp.int32, (N_SEL, n_cmp), 1) * CMP_STRIDE
    ss = lax.broadcasted_iota(jnp.int32, (N_SEL, n_cmp), 0) * SEL_BLOCK
    overlap = jnp.maximum(jnp.minimum(cs + CMP_BLOCK, ss + SEL_BLOCK) - jnp.maximum(cs, ss), 0)
    overlap = overlap.astype(f32) * (1.0 / CMP_BLOCK)
    jrow = lax.broadcasted_iota(jnp.int32, (N_SEL, tq), 0)

    def qblock(i, carry):
        t0 = pl.multiple_of(i * tq, tq)
        q = q_ref[0, :, pl.ds(t0, tq), :].reshape(rows, hd)
        tpos = t0 + lax.broadcasted_iota(jnp.int32, (n_head, tq, 1), 1).reshape(rows, 1)
        dist = tpos - ends
        s = lax.dot_general(q, kc_scr[...], (((1,), (1,)), ((), ())), preferred_element_type=f32) * scale
        s = s - slope * dist.astype(f32)
        valid = dist >= 0
        s = jnp.where(valid, s, NEG_BIG)
        m = jnp.max(s, axis=1, keepdims=True)
        e = jnp.where(valid, jnp.exp(s - m), 0.0)
        p = e / jnp.maximum(jnp.sum(e, axis=1, keepdims=True), 1e-30)
        o = jnp.dot(p.astype(bf16), vc_scr[...], preferred_element_type=f32)
        o_ref[0, pl.ds(t0, tq), :] = _gated_heads(o.reshape(n_head, tq, hd), gl_ref[0, :, pl.ds(t0, tq), :])
        psum = jnp.sum(p.reshape(n_head, tq, n_cmp), axis=0)
        imp = lax.dot_general(overlap, psum, (((1,), (1,)), ((), ())), preferred_element_type=f32,
                              precision=lax.Precision.HIGHEST)
        tcol = t0 + lax.broadcasted_iota(jnp.int32, (N_SEL, tq), 1)
        cur = tcol // SEL_BLOCK
        forced = (jrow == 0) | (jrow == cur) | (jrow == cur - 1)
        imp = jnp.where(jrow * SEL_BLOCK <= tcol, imp + jnp.where(forced, FORCE_BONUS, 0.0), -jnp.inf)
        rank = jnp.zeros((N_SEL, tq), jnp.int32)
        for j in range(N_SEL):
            other = imp[j:j + 1]
            ahead = (other > imp) | ((other == imp) & (j < jrow))
            rank = rank + ahead.astype(jnp.int32)
        sel_ref[0, 0, pl.ds(t0, tq), :] = (rank < SEL_TOPK).astype(f32).T
        return carry

    lax.fori_loop(0, L // tq, qblock, 0)


def _nsa_cmp(q, kseg, vseg, gl, wk, wv, bk, bv, w2k, w2v):
    b, h, L, hd = q.shape
    g = kseg.shape[1]
    r = h // g
    full = lambda a: pl.BlockSpec(a.shape, lambda bi, gi: (0,) * a.ndim)
    seg_spec = pl.BlockSpec((1, 1) + kseg.shape[2:], lambda bi, gi: (bi, gi, 0, 0))
    return pl.pallas_call(
        _nsa_cmp_kernel,
        grid=(b, g),
        in_specs=[pl.BlockSpec((1, r, L, hd), lambda bi, gi: (bi, gi, 0, 0)), seg_spec, seg_spec,
                  pl.BlockSpec((1, r, L, 1), lambda bi, gi: (bi, gi, 0, 0)),
                  full(wk), full(wv), full(bk), full(bv), full(w2k), full(w2v)],
        out_specs=[pl.BlockSpec((1, L, r * hd), lambda bi, gi: (bi, 0, gi)),
                   pl.BlockSpec((1, 1, L, N_SEL), lambda bi, gi: (bi, gi, 0, 0))],
        out_shape=[jax.ShapeDtypeStruct((b, L, h * hd), jnp.float32),
                   jax.ShapeDtypeStruct((b, g, L, N_SEL), jnp.float32)],
        scratch_shapes=[pltpu.VMEM((N_CMP_PAD, hd), jnp.bfloat16), pltpu.VMEM((N_CMP_PAD, hd), jnp.bfloat16)],
        compiler_params=pltpu.CompilerParams(dimension_semantics=("parallel", "arbitrary"),
                                             vmem_limit_bytes=VMEM_LIMIT),
        name="nsa_cmp",
    )(q, kseg, vseg, gl, wk, wv, bk, bv, w2k, w2v)


def nsa_mixer(x, w_in, pe_k, w1_k, w2_k, pe_v, w1_v, w2_v, w_out, ln_g, ln_b):
    bf16 = jnp.bfloat16
    bsz, L, D = x.shape
    G, hd = NSA_KV_HEADS, NSA_HEAD_DIM
    x2 = x.reshape(bsz * L, D)
    proj = _matmul(x2, w_in).reshape(bsz, L, -1)
    q = _heads_first(proj[..., :NSA_Q_WIDTH], NSA_HEADS)
    n_kv = 2 * NSA_BRANCHES
    kv = proj[..., NSA_Q_WIDTH:NSA_Q_WIDTH + n_kv * NSA_KV_WIDTH]
    kv = kv.reshape(bsz, L, n_kv, G, hd).transpose(2, 0, 3, 1, 4).astype(bf16)
    kc, vc, ks, vs, kw, vw = [kv[j] for j in range(n_kv)]
    gl = proj[..., NSA_Q_WIDTH + n_kv * NSA_KV_WIDTH:].reshape(bsz, L, NSA_BRANCHES, NSA_HEADS)
    gl = gl.transpose(2, 0, 3, 1)[..., None]

    def halves(w1):
        return w1.reshape(2, SEG * hd, -1).astype(bf16)

    def pos_term(pe, w1):
        return jnp.dot(pe.reshape(1, CMP_BLOCK * hd), w1, precision="highest")

    kseg = kc.reshape(bsz, G, L // SEG, SEG * hd)
    vseg = vc.reshape(bsz, G, L // SEG, SEG * hd)
    o_cmp, sel = _nsa_cmp(q, kseg, vseg, gl[0], halves(w1_k), halves(w1_v), pos_term(pe_k, w1_k),
                          pos_term(pe_v, w1_v), w2_k.astype(bf16), w2_v.astype(bf16))
    o_slc = _nsa_attn(q, ks, vs, gl[1], sel, "sel")
    o_win = _nsa_attn(q, kw, vw, gl[2], None, "win")
    branches = [o.reshape(bsz * L, NSA_Q_WIDTH) for o in (o_cmp, o_slc, o_win)]
    return _proj_res_ln([(branches, w_out.astype(bf16))], x2, ln_g, ln_b).reshape(bsz, L, D)


PEER_TC = LANE
PEER_SEL = PEER_HEADS * PEER_TOPK
ROW_TILES = D_MODEL // LANE
HI_HALF = 0xFFFF0000
ROUTE_OFF_MASK = 0x7FFFFFFF
PEER_TOK_BATCH = 8
PEER_STEP = 32
STEP_WORDS = PEER_STEP * PEER_SEL


def _top16(s):
    n = s.shape[0]
    iota = lax.broadcasted_iota(jnp.int32, s.shape, 0)
    vals, idxs = [], []
    for _ in range(PEER_TOPK):
        m = jnp.max(s, axis=0, keepdims=True)
        idx = jnp.min(jnp.where(s == m, iota, n), axis=0, keepdims=True)
        vals.append(m)
        idxs.append(idx)
        s = jnp.where(iota == idx, -jnp.inf, s)
    return jnp.concatenate(vals, axis=0), jnp.concatenate(idxs, axis=0)


def _peer_topk_kernel(x_ref, wq_ref, keys_ref, off_ref, sh_ref, code_ref, g_ref):
    f32 = jnp.float32
    t = x_ref.shape[0]
    xb = x_ref[...].astype(jnp.bfloat16)
    j8 = lax.broadcasted_iota(jnp.int32, (8, t), 0)
    j16 = lax.broadcasted_iota(jnp.int32, (16, t), 0)
    neg = -jnp.inf
    experts, gates = [], []
    for h in range(PEER_HEADS):
        q = jnp.dot(xb, wq_ref[:, h * PEER_KEY_DIM:(h + 1) * PEER_KEY_DIM],
                    preferred_element_type=f32).astype(jnp.bfloat16)
        dn = (((1,), (1,)), ((), ()))
        s1 = lax.dot_general(keys_ref[2 * h], q[:, :PEER_HALF], dn, preferred_element_type=f32)
        s2 = lax.dot_general(keys_ref[2 * h + 1], q[:, PEER_HALF:], dn, preferred_element_type=f32)
        v1, i1 = _top16(s1)
        v2, i2 = _top16(s2)
        cand = [v1[0:1] + v2]
        flat = [j16]
        pay = [i1[0:1] * PEER_KEYS + i2]
        for i in range(1, 8):
            keep = j8 < PEER_TOPK // (i + 1)
            cand.append(jnp.where(keep, v1[i:i + 1] + v2[0:8], neg))
            flat.append(i * PEER_TOPK + j8)
            pay.append(i1[i:i + 1] * PEER_KEYS + i2[0:8])
        cand.append(v1[8:16] + v2[0:1])
        flat.append((8 + j8) * PEER_TOPK)
        pay.append(i1[8:16] * PEER_KEYS + i2[0:1])
        cand = jnp.concatenate(cand, axis=0)
        flat = jnp.concatenate(flat, axis=0)
        pay = jnp.concatenate(pay, axis=0)
        tops = []
        for _ in range(PEER_TOPK):
            m = jnp.max(cand, axis=0, keepdims=True)
            fl = jnp.min(jnp.where(cand == m, flat, PEER_TOPK * PEER_TOPK), axis=0, keepdims=True)
            sel = flat == fl
            experts.append(jnp.max(jnp.where(sel, pay, -1), axis=0, keepdims=True))
            tops.append(m)
            cand = jnp.where(sel, neg, cand)
        top = jnp.concatenate(tops, axis=0)
        e = jnp.exp(top - top[0:1])
        gates.append(e / jnp.sum(e, axis=0, keepdims=True))
    expert = jnp.concatenate(experts, axis=0)
    off = ((expert >> 1) * ROW_TILES).T
    low = ((expert & 1) ^ 1).T
    off_ref[...] = off
    sh_ref[...] = low * 16
    code_ref[...] = off | (low << 31)
    g_ref[...] = jnp.concatenate(gates, axis=0).T


def _peer_topk(x2, wq_b, keys_b):
    n_tok, d = x2.shape
    blk = pl.BlockSpec((PEER_TC, PEER_SEL), lambda i: (i, 0))
    ints = jax.ShapeDtypeStruct((n_tok, PEER_SEL), jnp.int32)
    return pl.pallas_call(
        _peer_topk_kernel,
        grid=(n_tok // PEER_TC,),
        in_specs=[pl.BlockSpec((PEER_TC, d), lambda i: (i, 0)),
                  pl.BlockSpec(wq_b.shape, lambda i: (0, 0)),
                  pl.BlockSpec(keys_b.shape, lambda i: (0, 0, 0))],
        out_specs=[blk, blk, blk, blk],
        out_shape=[ints, ints, ints, jax.ShapeDtypeStruct((n_tok, PEER_SEL), jnp.float32)],
        compiler_params=pltpu.CompilerParams(dimension_semantics=("arbitrary",), vmem_limit_bytes=VMEM_LIMIT),
        name="peer_topk",
    )(x2, wq_b, keys_b)


def _pack_table(tab):
    n, d = tab.shape
    b = lax.bitcast_convert_type(tab.astype(jnp.bfloat16), jnp.uint16).astype(jnp.uint32)
    b = b.reshape(n // 2, 2, d)
    return (b[:, 0] | (b[:, 1] << 16)).reshape(n // 2 * (d // LANE), LANE)


def _rowsum8(ps):
    sub = lax.broadcasted_iota(jnp.int32, (8, LANE), 0)
    for d in (1, 2, 4):
        keep = (sub & (2 * d - 1)) < d
        ps = [jnp.where(keep, a, b) + pltpu.roll(jnp.where(keep, b, a), d, 0)
              for a, b in zip(ps[0::2], ps[1::2])]
    return ps[0]


def _table_row(tab_ref, off, sh):
    t = tab_ref[pl.ds(pl.multiple_of(off, ROW_TILES), ROW_TILES), :]
    return pltpu.bitcast((t << sh) & jnp.uint32(HI_HALF), jnp.float32)


def _splat(word, dtype):
    return jnp.full((ROW_TILES, LANE), word, dtype)


def _peer_gather_step(hbm_refs, smem_refs, sem, tab_hbm, tab_vmem, sem_tab, body):
    i = pl.program_id(0)
    n = pl.num_programs(0)

    def copies(step, slot):
        return [pltpu.make_async_copy(h.at[pl.ds(step * STEP_WORDS, STEP_WORDS)],
                                      m.at[pl.ds(slot * STEP_WORDS, STEP_WORDS)], sem.at[k, slot])
                for k, (h, m) in enumerate(zip(hbm_refs, smem_refs))]

    @pl.when(i == 0)
    def _():
        tab = pltpu.make_async_copy(tab_hbm, tab_vmem, sem_tab)
        tab.start()
        for c in copies(0, 0):
            c.start()
        tab.wait()

    for slot in range(2):
        @pl.when(i % 2 == slot)
        def _(slot=slot):
            @pl.when(i + 1 < n)
            def _():
                for c in copies(i + 1, 1 - slot):
                    c.start()

            for c in copies(i, slot):
                c.wait()
            body(slot * STEP_WORDS)


def _peer_u_kernel(off_hbm, sh_hbm, tab_hbm, x_ref, g_ref, w_ref, tab_vmem, off_s, sh_s, sem, sem_tab):
    lane = lax.broadcasted_iota(jnp.int32, (PEER_SEL, LANE), 1)

    def body(base):
        hacc = jnp.zeros((PEER_SEL, LANE), jnp.float32)
        for cb in range(PEER_STEP // PEER_TOK_BATCH):
            sums = []
            for j in range(PEER_TOK_BATCH):
                c = cb * PEER_TOK_BATCH + j
                xt = x_ref[c]
                word = base + c * PEER_SEL
                for b in range(PEER_SEL // 8):
                    sums.append(_rowsum8([_table_row(tab_vmem, off_s[word + b * 8 + r],
                                                     _splat(sh_s[word + b * 8 + r], jnp.uint32)) * xt
                                          for r in range(8)]))
            col = jnp.sum(jnp.concatenate(sums, axis=0), axis=1, keepdims=True)
            for j in range(PEER_TOK_BATCH):
                hacc = jnp.where(lane == cb * PEER_TOK_BATCH + j, col[j * PEER_SEL:(j + 1) * PEER_SEL], hacc)
        w_ref[...] = g_ref[...] * jax.nn.gelu(hacc.T[:PEER_STEP])

    _peer_gather_step((off_hbm, sh_hbm), (off_s, sh_s), sem, tab_hbm, tab_vmem, sem_tab, body)


def _peer_v_kernel(code_hbm, w_hbm, tab_hbm, o_ref, tab_vmem, code_s, w_s, sem, sem_tab):
    n_acc = 4

    def body(base):
        for c in range(PEER_STEP):
            accs = [None] * n_acc
            for r in range(PEER_SEL):
                code = code_s[base + c * PEER_SEL + r]
                sh = jnp.where(_splat(code, jnp.int32) < 0, jnp.uint32(16), jnp.uint32(0))
                term = _table_row(tab_vmem, code & ROUTE_OFF_MASK, sh) * w_s[base + c * PEER_SEL + r]
                accs[r % n_acc] = term if accs[r % n_acc] is None else accs[r % n_acc] + term
            o_ref[c] = (accs[0] + accs[1]) + (accs[2] + accs[3])

    _peer_gather_step((code_hbm, w_hbm), (code_s, w_s), sem, tab_hbm, tab_vmem, sem_tab, body)


def _peer_gather_params():
    return pltpu.CompilerParams(dimension_semantics=("arbitrary",), vmem_limit_bytes=56 * 1024 * 1024)


def _peer_u(off, sh, tab_u, x3, g):
    n_tok = g.shape[0]
    any_spec = pl.BlockSpec(memory_space=pl.ANY)
    tok = pl.BlockSpec((PEER_STEP, PEER_SEL), lambda i: (i, 0))
    smem = pltpu.SMEM((2 * STEP_WORDS,), jnp.int32)
    return pl.pallas_call(
        _peer_u_kernel,
        grid=(n_tok // PEER_STEP,),
        in_specs=[any_spec, any_spec, any_spec,
                  pl.BlockSpec((PEER_STEP, ROW_TILES, LANE), lambda i: (i, 0, 0)), tok],
        out_specs=tok,
        out_shape=jax.ShapeDtypeStruct((n_tok, PEER_SEL), jnp.float32),
        scratch_shapes=[pltpu.VMEM(tab_u.shape, jnp.uint32), smem, smem,
                        pltpu.SemaphoreType.DMA((2, 2)), pltpu.SemaphoreType.DMA(())],
        compiler_params=_peer_gather_params(),
        name="peer_u",
    )(off, sh, tab_u, x3, g)


def _peer_v(code, w, tab_v):
    n_tok = code.shape[0] // PEER_SEL
    any_spec = pl.BlockSpec(memory_space=pl.ANY)
    smem = pltpu.SMEM((2 * STEP_WORDS,), jnp.int32)
    return pl.pallas_call(
        _peer_v_kernel,
        grid=(n_tok // PEER_STEP,),
        in_specs=[any_spec, any_spec, any_spec],
        out_specs=pl.BlockSpec((PEER_STEP, ROW_TILES, LANE), lambda i: (i, 0, 0)),
        out_shape=jax.ShapeDtypeStruct((n_tok, ROW_TILES, LANE), jnp.float32),
        scratch_shapes=[pltpu.VMEM(tab_v.shape, jnp.uint32), smem,
                        pltpu.SMEM((2 * STEP_WORDS,), jnp.float32),
                        pltpu.SemaphoreType.DMA((2, 2)), pltpu.SemaphoreType.DMA(())],
        compiler_params=_peer_gather_params(),
        name="peer_v",
    )(code, w, tab_v)


def peer_ffn(x, w_q, sub_keys, u_tab, v_tab, ln_g, ln_b):
    bsz, L, D = x.shape
    x2 = x.reshape(bsz * L, D)
    keys_b = sub_keys.reshape(PEER_HEADS * 2, PEER_KEYS, PEER_HALF).astype(jnp.bfloat16)
    off, sh, code, g = _peer_topk(x2, w_q.astype(jnp.bfloat16), keys_b)
    w = _peer_u(off.reshape(-1), sh.reshape(-1), _pack_table(u_tab), x2.reshape(bsz * L, ROW_TILES, LANE), g)
    out = _peer_v(code.reshape(-1), w.reshape(-1), _pack_table(v_tab))
    return _res_ln(x2, out.reshape(bsz * L, D), ln_g, ln_b).reshape(bsz, L, D)


def kernel(x, even_w_in, even_lam_re, even_lam_im, even_log_dt, even_b_re, even_b_im, even_c_re, even_c_im, even_d, even_w_glu, even_b_glu, even_w_out, odd_w_in, odd_pe_k, odd_w1_k, odd_w2_k, odd_pe_v, odd_w1_v, odd_w2_v, odd_w_out, peer_w_q, peer_sub_keys, peer_u, peer_v, ln1_g, ln1_b, ln2_g, ln2_b):
    h = x
    for i in range(DEPTH):
        j = i // 2
        if i % 2 == 0:
            h = even_mixer(h, even_w_in[j], even_lam_re[j], even_lam_im[j], even_log_dt[j],
                           even_b_re[j], even_b_im[j], even_c_re[j], even_c_im[j], even_d[j],
                           even_w_glu[j], even_b_glu[j], even_w_out[j], ln1_g[i], ln1_b[i])
        else:
            h = nsa_mixer(h, odd_w_in[j], odd_pe_k[j], odd_w1_k[j], odd_w2_k[j],
                          odd_pe_v[j], odd_w1_v[j], odd_w2_v[j], odd_w_out[j], ln1_g[i], ln1_b[i])
        h = peer_ffn(h, peer_w_q[i], peer_sub_keys[i], peer_u[i], peer_v[i], ln2_g[i], ln2_b[i])
    return h
```
